```python
import math, functools
import jax, jax.numpy as jnp
from jax import lax
import numpy as np

D_MODEL = 1024
BATCH = 4
SEQ = 8192
DEPTH = 2
DEC_BATCH = 32
DEC_SEQ = 8
PAST_LEN = 16384
PAGE_SIZE = 128

D_MIX = D_MODEL
D_REC = D_MIX // 2
N_REC_BLOCKS = 8
REC_BLOCK = D_REC // N_REC_BLOCKS
CONV_W = 4
RG_C = 8.0
D_ATT = D_MIX - D_REC
N_HEADS = 8
HEAD_DIM = D_ATT // N_HEADS
Q_BLOCK = 128
D_IN = 2 * D_REC + 4 * D_ATT
SB_BIAS_INIT = -6.0
LN_EPS = 1e-5
DEEPNORM_ALPHA = (2 * DEPTH) ** 0.25
DEEPNORM_BETA = (8 * DEPTH) ** -0.25

kernel_name = 'hymba_rglru_stickbreak_adaln_deepnorm_step'


def _layer_norm(x):
    xf = x.astype(jnp.float32)
    mu = jnp.mean(xf, axis=-1, keepdims=True)
    var = jnp.mean(jnp.square(xf - mu), axis=-1, keepdims=True)
    return (xf - mu) * lax.rsqrt(var + LN_EPS)


def _adaln(x, c, w_ada, b_ada):
    mod = jnp.matmul(jax.nn.silu(c.astype(jnp.float32)), w_ada) + b_ada
    shift, scale, gate = jnp.split(mod, 3, axis=-1)
    u = _layer_norm(x) * (1.0 + scale[:, None, :]) + shift[:, None, :]
    return u.astype(x.dtype), gate


def _rglru(xa, pos, conv_buf, h0, conv_w, conv_b, w_r, b_r, w_i, b_i, lam):
    B, T, _ = xa.shape
    xp = jnp.concatenate([conv_buf.astype(xa.dtype), xa], axis=1)
    xc = conv_b + sum(xp[:, j:j + T] * conv_w[j] for j in range(CONV_W))
    new_buf = xp[:, T:]
    xcf = xc.astype(jnp.float32)
    xb = xcf.reshape(B, T, N_REC_BLOCKS, REC_BLOCK)
    r = jax.nn.sigmoid(jnp.einsum('btnd,nde->btne', xb, w_r).reshape(B, T, D_REC) + b_r)
    i = jax.nn.sigmoid(jnp.einsum('btnd,nde->btne', xb, w_i).reshape(B, T, D_REC) + b_i)
    log_a = -RG_C * jax.nn.softplus(-lam.astype(jnp.float32)) * r
    a = jnp.exp(log_a)
    mult = jnp.sqrt(-jnp.expm1(2.0 * log_a))
    mult = jnp.where((pos == 0)[None, :, None], 1.0, mult)
    b = mult * (i * xcf)
    b = b.at[:, 0].add(a[:, 0] * h0.astype(jnp.float32))

    def combine(left, right):
        a_l, b_l = left
        a_r, b_r_ = right
        return a_l * a_r, a_r * b_l + b_r_

    _, h = lax.associative_scan(combine, (a, b), axis=1)
    return h, new_buf, h[:, -1]


def _stick_breaking(q, k, v, sb_bias, q_pos, k_pos):
    z = jnp.einsum('bqhd,bkhd->bhqk', q.astype(jnp.float32), k.astype(jnp.float32)) / math.sqrt(HEAD_DIM)
    z = z + sb_bias.astype(jnp.float32)[None, :, None, None]
    mask = k_pos[None, :] < q_pos[:, None]
    log_fail = jnp.where(mask, -jax.nn.softplus(z), 0.0)
    log_later = lax.cumsum(log_fail, axis=3, reverse=True) - log_fail
    w = jnp.where(mask, jnp.exp(jax.nn.log_sigmoid(z) + log_later), 0.0)
    return jnp.einsum('bhqk,bkhd->bqhd', w, v.astype(jnp.float32))


def _prompt_attention(q, k, v, sb_bias):
    B, S, H, Dh = q.shape
    nblk = S // Q_BLOCK
    qb = q.reshape(B, nblk, Q_BLOCK, H, Dh).transpose(1, 0, 2, 3, 4)
    qpos = jnp.arange(S, dtype=jnp.int32).reshape(nblk, Q_BLOCK)
    kpos = jnp.arange(S, dtype=jnp.int32)
    ob = lax.map(lambda args: _stick_breaking(args[0], k, v, sb_bias, args[1], kpos), (qb, qpos))
    return ob.transpose(1, 0, 2, 3, 4).reshape(B, S, H, Dh)


def _sample_attention(q, k, v, sb_bias, k_past, v_past, q_pos, k_pos):
    k_all = jnp.concatenate([k_past.astype(k.dtype), k], axis=1)
    v_all = jnp.concatenate([v_past.astype(v.dtype), v], axis=1)
    return _stick_breaking(q, k_all, v_all, sb_bias, q_pos, k_pos)


def _mixer_layer(x, c, pos, conv_buf, h0, w_ada, b_ada, w_in, conv_w, conv_b, w_r, b_r, w_i, b_i,
                 lam, sb_bias, w_out, ln_g, ln_b, attend):
    B, T, _ = x.shape
    u, gate = _adaln(x, c, w_ada, b_ada)
    z = jnp.einsum('btd,de->bte', u, w_in)
    o1 = D_REC
    o2 = 2 * D_REC
    o3 = o2 + D_ATT
    o4 = o3 + D_ATT
    o5 = o4 + D_ATT
    xa, ga = z[..., :o1], z[..., o1:o2]
    q = z[..., o2:o3].reshape(B, T, N_HEADS, HEAD_DIM)
    k = z[..., o3:o4].reshape(B, T, N_HEADS, HEAD_DIM)
    v = z[..., o4:o5].reshape(B, T, N_HEADS, HEAD_DIM)
    gb = z[..., o5:]
    h, new_buf, h_last = _rglru(xa, pos, conv_buf, h0, conv_w, conv_b, w_r, b_r, w_i, b_i, lam)
    o = attend(q, k, v, sb_bias)
    y_rec = h * jax.nn.silu(ga.astype(jnp.float32))
    y_att = o.reshape(B, T, D_ATT) * jax.nn.silu(gb.astype(jnp.float32))
    y = jnp.einsum('btm,md->btd', jnp.concatenate([y_rec, y_att], axis=-1), w_out) * gate[:, None, :]
    x_new = _layer_norm(DEEPNORM_ALPHA * x.astype(jnp.float32) + y) * ln_g + ln_b
    return x_new.astype(x.dtype), k, v, new_buf, h_last


def setup_inputs(seed: int = 0) -> dict:
    key = jax.random.key(seed)
    ks = jax.random.split(key, 24)
    f32 = jnp.float32
    n_pages = PAST_LEN // PAGE_SIZE
    n_used = DEC_BATCH * n_pages
    n_pool = n_used + (n_used + 3) // 4

    def nrm(k, shape, s=1.0):
        return s * jax.random.normal(k, shape, f32)

    x_prompt = nrm(ks[0], (BATCH, SEQ, D_MODEL))
    x_sample = nrm(ks[1], (DEC_BATCH, DEC_SEQ, D_MODEL))
    cache_k = nrm(ks[2], (DEPTH, n_pool, PAGE_SIZE, N_HEADS, HEAD_DIM))
    cache_v = nrm(ks[3], (DEPTH, n_pool, PAGE_SIZE, N_HEADS, HEAD_DIM))
    state_h = nrm(ks[4], (DEPTH, DEC_BATCH, D_REC))
    state_conv = nrm(ks[5], (DEPTH, DEC_BATCH, CONV_W - 1, D_REC))
    page_table = jax.random.permutation(ks[6], n_pool)[:n_used].reshape(DEC_BATCH, n_pages).astype(jnp.int32)
    c_prompt = nrm(ks[7], (BATCH, D_MODEL))
    c_sample = nrm(ks[8], (DEC_BATCH, D_MODEL))
    w_ada = nrm(ks[9], (DEPTH, D_MODEL, 3 * D_MODEL), D_MODEL ** -0.5)
    b_ada = nrm(ks[10], (DEPTH, 3 * D_MODEL), 0.01)
    w_in = nrm(ks[11], (DEPTH, D_MODEL, D_IN), D_MODEL ** -0.5)
    conv_w = nrm(ks[12], (DEPTH, CONV_W, D_REC), CONV_W ** -0.5)
    conv_b = nrm(ks[13], (DEPTH, D_REC), 0.01)
    w_r = nrm(ks[14], (DEPTH, N_REC_BLOCKS, REC_BLOCK, REC_BLOCK), REC_BLOCK ** -0.5)
    b_r = nrm(ks[15], (DEPTH, D_REC), 0.01)
    w_i = nrm(ks[16], (DEPTH, N_REC_BLOCKS, REC_BLOCK, REC_BLOCK), REC_BLOCK ** -0.5)
    b_i = nrm(ks[17], (DEPTH, D_REC), 0.01)
    a_pow = jax.random.uniform(ks[18], (DEPTH, D_REC), f32, minval=0.9, maxval=0.999)
    a_base = a_pow ** (1.0 / RG_C)
    lam = jnp.log(a_base) - jnp.log1p(-a_base)
    sb_bias = SB_BIAS_INIT + nrm(ks[22], (DEPTH, N_HEADS), 0.1)
    w_out = nrm(ks[19], (DEPTH, D_MIX, D_MODEL), DEEPNORM_BETA * D_MIX ** -0.5)
    ln_g = 1.0 + nrm(ks[20], (DEPTH, D_MODEL), 0.01)
    ln_b = nrm(ks[21], (DEPTH, D_MODEL), 0.01)
    return {'x_prompt': x_prompt, 'x_sample': x_sample, 'cache_k': cache_k, 'cache_v': cache_v,
            'state_h': state_h, 'state_conv': state_conv, 'page_table': page_table,
            'c_prompt': c_prompt, 'c_sample': c_sample, 'w_ada': w_ada, 'b_ada': b_ada, 'w_in': w_in,
            'conv_w': conv_w, 'conv_b': conv_b, 'w_r': w_r, 'b_r': b_r, 'w_i': w_i, 'b_i': b_i,
            'lam': lam, 'sb_bias': sb_bias, 'w_out': w_out, 'ln_g': ln_g, 'ln_b': ln_b}


def reference(x_prompt, x_sample, cache_k, cache_v, state_h, state_conv, page_table, c_prompt, c_sample,
              w_ada, b_ada, w_in, conv_w, conv_b, w_r, b_r, w_i, b_i, lam, sb_bias, w_out, ln_g, ln_b):
    B, S, _ = x_prompt.shape
    Bd, T, _ = x_sample.shape
    n_pages = page_table.shape[1]
    past = n_pages * PAGE_SIZE
    pos_p = jnp.arange(S, dtype=jnp.int32)
    pos_s = past + jnp.arange(T, dtype=jnp.int32)
    k_pos_s = jnp.arange(past + T, dtype=jnp.int32)
    xp, xs = x_prompt, x_sample
    kp_l, vp_l, hp_l, cp_l, ks_l, vs_l, hs_l, cs_l = [], [], [], [], [], [], [], []
    for l in range(DEPTH):
        params = (w_ada[l], b_ada[l], w_in[l], conv_w[l], conv_b[l], w_r[l], b_r[l], w_i[l], b_i[l],
                  lam[l], sb_bias[l], w_out[l], ln_g[l], ln_b[l])
        xp, kp, vp, bufp, hp = _mixer_layer(
            xp, c_prompt, pos_p, jnp.zeros((B, CONV_W - 1, D_REC), xp.dtype), jnp.zeros((B, D_REC), jnp.float32),
            *params, attend=_prompt_attention)
        k_past = cache_k[l][page_table].reshape(Bd, past, N_HEADS, HEAD_DIM)
        v_past = cache_v[l][page_table].reshape(Bd, past, N_HEADS, HEAD_DIM)
        attend_s = functools.partial(_sample_attention, k_past=k_past, v_past=v_past, q_pos=pos_s, k_pos=k_pos_s)
        xs, ksm, vsm, bufs, hs = _mixer_layer(xs, c_sample, pos_s, state_conv[l], state_h[l], *params, attend=attend_s)
        kp_l.append(kp); vp_l.append(vp); hp_l.append(hp); cp_l.append(bufp)
        ks_l.append(ksm); vs_l.append(vsm); hs_l.append(hs); cs_l.append(bufs)
    k_prompt = jnp.stack(kp_l)
    v_prompt = jnp.stack(vp_l)
    h_prompt = jnp.stack(hp_l)
    conv_prompt = jnp.stack(cp_l)
    k_sample = jnp.stack(ks_l)
    v_sample = jnp.stack(vs_l)
    h_sample = jnp.stack(hs_l)
    conv_sample = jnp.stack(cs_l)
    return (xp, xs, k_prompt, v_prompt, h_prompt, conv_prompt, k_sample, v_sample, h_sample, conv_sample)
```

```python
import functools
import math

import jax
import jax.numpy as jnp
from jax import lax
from jax.experimental import pallas as pl
from jax.experimental.pallas import tpu as pltpu

F32 = jnp.float32
BF16 = jnp.bfloat16

N_HEADS = 8
HEAD_DIM = 64
N_REC_BLOCKS = 8
CONV_W = 4
RG_C = 8.0
PAGE_SIZE = 128
LN_EPS = 1e-5

SUBLANES = 8
LANES = 128
VMEM_LIMIT = 56 * 1024 * 1024

PRE_ROWS = 256
POST_ROWS = 512
ATT_BLOCK = 256
PAGES_PER_STEP = 8


def _sigmoid(x):
    return 1.0 / (1.0 + jnp.exp(-x))


def _neg_softplus(z):
    return jnp.minimum(-z, 0.0) - jnp.log(1.0 + jnp.exp(-jnp.abs(z)))


def _layer_norm(x):
    mu = jnp.mean(x, axis=-1, keepdims=True)
    xc = x - mu
    var = jnp.mean(xc * xc, axis=-1, keepdims=True)
    return xc * lax.rsqrt(var + LN_EPS)


def _params(n_axes):
    return pltpu.CompilerParams(dimension_semantics=("arbitrary",) * n_axes,
                                vmem_limit_bytes=VMEM_LIMIT)


def _mod_kernel(c_ref, w_ref, b_ref, o_ref):
    c = c_ref[...]
    sc = c * _sigmoid(c)
    o_ref[0] = jnp.dot(sc, w_ref[0], precision=lax.Precision.HIGHEST,
                       preferred_element_type=F32) + b_ref[0]


def _modulation(c_all, w_ada, b_ada):
    depth, d, d3 = w_ada.shape
    rows = c_all.shape[0]
    n_col = d3 // d
    return pl.pallas_call(
        _mod_kernel,
        grid=(depth, n_col),
        in_specs=[
            pl.BlockSpec((rows, d), lambda l, n: (0, 0)),
            pl.BlockSpec((1, d, d), lambda l, n: (l, 0, n)),
            pl.BlockSpec((1, 1, d), lambda l, n: (l, 0, n)),
        ],
        out_specs=pl.BlockSpec((1, rows, d), lambda l, n: (l, 0, n)),
        out_shape=jax.ShapeDtypeStruct((depth, rows, d3), F32),
        compiler_params=_params(2),
        name="adaln_modulation",
    )(c_all, w_ada, b_ada.reshape(depth, 1, d3))


def _pre_kernel(x_ref, shift_ref, scale_ref, win_ref, cw_ref, cb_ref, wg_ref, bg_ref, lam_ref,
                hist0_ref, h0_ref,
                k_ref, v_ref, qa_ref, ka_ref, va_ref, yrec_ref, sgb_ref, hlast_ref, conv_ref,
                xp_scr, h_scr, a_scr, b_scr, sga_scr, *, rows, d_rec, d_att, reset_first):
    ti = pl.program_id(1)

    @pl.when(ti == 0)
    def _():
        xp_scr[0:SUBLANES, :] = hist0_ref[0]
        h_scr[...] = jnp.broadcast_to(h0_ref[0], (SUBLANES, d_rec))

    x = x_ref[0]
    u = _layer_norm(x) * (1.0 + scale_ref[0]) + shift_ref[0]
    z = jnp.dot(u.astype(BF16), win_ref[...], preferred_element_type=F32)
    o1, o2 = d_rec, 2 * d_rec
    o3, o4, o5 = o2 + d_att, o2 + 2 * d_att, o2 + 3 * d_att
    xa = z[:, :o1]
    ga = z[:, o1:o2]
    q = z[:, o2:o3]
    k = z[:, o3:o4]
    v = z[:, o4:o5]
    gb = z[:, o5:]
    k_ref[0] = k
    v_ref[0] = v
    qa_ref[0] = (q * (1.0 / math.sqrt(HEAD_DIM))).astype(qa_ref.dtype)
    ka_ref[0] = k.astype(BF16)
    va_ref[0] = v.astype(BF16)
    sgb_ref[0] = gb * _sigmoid(gb)
    sga_scr[...] = ga * _sigmoid(ga)

    xp_scr[SUBLANES:SUBLANES + rows, :] = xa
    cw = cw_ref[...]
    xc = cb_ref[...] + cw[3:4, :] * xa
    for j in range(CONV_W - 1):
        off = SUBLANES - (CONV_W - 1) + j
        xc = xc + cw[j:j + 1, :] * xp_scr[off:off + rows, :]
    last_rows = xa[rows - SUBLANES:, :]
    xp_scr[0:SUBLANES, :] = last_rows
    conv_ref[0] = last_rows

    gates = jnp.dot(xc.astype(BF16), wg_ref[...], preferred_element_type=F32) + bg_ref[...]
    r = _sigmoid(gates[:, :d_rec])
    i = _sigmoid(gates[:, d_rec:])
    lam = lam_ref[...]
    c_lam = -RG_C * (jnp.maximum(-lam, 0.0) + jnp.log1p(jnp.exp(-jnp.abs(lam))))
    a = jnp.exp(c_lam * r)
    mult = jnp.sqrt(1.0 - a * a)
    if reset_first:
        row = lax.broadcasted_iota(jnp.int32, (rows, d_rec), 0)
        mult = jnp.where(jnp.logical_and(row == 0, ti == 0), 1.0, mult)
    a_scr[...] = a
    b_scr[...] = mult * (i * xc)

    srow = lax.broadcasted_iota(jnp.int32, (SUBLANES, d_rec), 0)

    def group(g, hprev):
        sl = pl.ds(pl.multiple_of(g * SUBLANES, SUBLANES), SUBLANES)
        ag = a_scr[sl, :]
        bg = b_scr[sl, :]
        for s in (1, 2, 4):
            keep = srow >= s
            a_sh = jnp.where(keep, pltpu.roll(ag, s, 0), 1.0)
            b_sh = jnp.where(keep, pltpu.roll(bg, s, 0), 0.0)
            bg = ag * b_sh + bg
            ag = ag * a_sh
        h = ag * hprev + bg
        b_scr[sl, :] = h
        return jnp.broadcast_to(h[SUBLANES - 1:SUBLANES, :], (SUBLANES, d_rec))

    hfin = lax.fori_loop(0, rows // SUBLANES, group, h_scr[...])
    h_scr[...] = hfin
    hlast_ref[0] = hfin[0:1, :]
    yrec_ref[0] = (b_scr[...] * sga_scr[...]).astype(BF16)


def _pre_call(x, shift, scale, win, cw, cb, wg, bg, lam, hist0, h0, *, rows, reset_first, q_dtype):
    bsz, seq, d = x.shape
    d_in = win.shape[1]
    d_rec = cw.shape[1]
    d_att = (d_in - 2 * d_rec) // 4
    nt = seq // rows
    kern = functools.partial(_pre_kernel, rows=rows, d_rec=d_rec, d_att=d_att, reset_first=reset_first)
    const2 = lambda b, t: (0, 0)
    per_b = lambda b, t: (b, 0, 0)
    tile = lambda b, t: (b, t, 0)
    out_shapes = (
        jax.ShapeDtypeStruct((bsz, seq, d_att), F32),
        jax.ShapeDtypeStruct((bsz, seq, d_att), F32),
        jax.ShapeDtypeStruct((bsz, seq, d_att), q_dtype),
        jax.ShapeDtypeStruct((bsz, seq, d_att), BF16),
        jax.ShapeDtypeStruct((bsz, seq, d_att), BF16),
        jax.ShapeDtypeStruct((bsz, seq, d_rec), BF16),
        jax.ShapeDtypeStruct((bsz, seq, d_att), F32),
        jax.ShapeDtypeStruct((bsz, 1, d_rec), F32),
        jax.ShapeDtypeStruct((bsz, SUBLANES, d_rec), F32),
    )
    out_specs = (
        pl.BlockSpec((1, rows, d_att), tile),
        pl.BlockSpec((1, rows, d_att), tile),
        pl.BlockSpec((1, rows, d_att), tile),
        pl.BlockSpec((1, rows, d_att), tile),
        pl.BlockSpec((1, rows, d_att), tile),
        pl.BlockSpec((1, rows, d_rec), tile),
        pl.BlockSpec((1, rows, d_att), tile),
        pl.BlockSpec((1, 1, d_rec), per_b),
        pl.BlockSpec((1, SUBLANES, d_rec), per_b),
    )
    in_specs = [
        pl.BlockSpec((1, rows, d), tile),
        pl.BlockSpec((1, 1, d), per_b),
        pl.BlockSpec((1, 1, d), per_b),
        pl.BlockSpec((d, d_in), const2),
        pl.BlockSpec((CONV_W, d_rec), const2),
        pl.BlockSpec((1, d_rec), const2),
        pl.BlockSpec((d_rec, 2 * d_rec), const2),
        pl.BlockSpec((1, 2 * d_rec), const2),
        pl.BlockSpec((1, d_rec), const2),
        pl.BlockSpec((1, SUBLANES, d_rec), per_b),
        pl.BlockSpec((1, 1, d_rec), per_b),
    ]
    return pl.pallas_call(
        kern,
        grid=(bsz, nt),
        in_specs=in_specs,
        out_specs=out_specs,
        out_shape=out_shapes,
        scratch_shapes=[
            pltpu.VMEM((SUBLANES + rows, d_rec), F32),
            pltpu.VMEM((SUBLANES, d_rec), F32),
            pltpu.VMEM((rows, d_rec), F32),
            pltpu.VMEM((rows, d_rec), F32),
            pltpu.VMEM((rows, d_rec), F32),
        ],
        compiler_params=_params(2),
        name="pre_mixer",
    )(x, shift, scale, win, cw, cb, wg, bg, lam, hist0, h0)


def _attn_kernel(bias_ref, q_ref, k_ref, v_ref, u_ref, o_ref, acc_scr, car_scr, *, blk):
    hp = pl.program_id(1)
    qi = pl.program_id(2)
    q = q_ref[0]
    lane = lax.broadcasted_iota(jnp.int32, (blk, LANES), 1)
    zero = jnp.zeros_like(q)
    qs = (jnp.where(lane < HEAD_DIM, q, zero), jnp.where(lane >= HEAD_DIM, q, zero))
    bias = (bias_ref[2 * hp], bias_ref[2 * hp + 1])
    u = u_ref[...]
    row = lax.broadcasted_iota(jnp.int32, (blk, blk), 0)
    col = lax.broadcasted_iota(jnp.int32, (blk, blk), 1)
    causal = col < row

    acc_scr[...] = jnp.zeros_like(acc_scr)
    car_scr[...] = jnp.zeros_like(car_scr)

    def block(kb, masked):
        start = pl.multiple_of(kb * blk, blk)
        kblk = k_ref[0, pl.ds(start, blk), :]
        vblk = v_ref[0, pl.ds(start, blk), :]
        for h in range(2):
            z = lax.dot_general(qs[h], kblk, (((1,), (1,)), ((), ())),
                                preferred_element_type=F32) + bias[h]
            lf = _neg_softplus(z)
            if masked:
                lf = jnp.where(causal, lf, 0.0)
            cs = jnp.dot(lf.astype(BF16), u, preferred_element_type=F32)
            carry = car_scr[h]
            cs = cs + jnp.concatenate([carry] * (blk // LANES), axis=1)
            w = jnp.exp(z + cs)
            if masked:
                w = jnp.where(causal, w, 0.0)
            acc_scr[h] += jnp.dot(w.astype(BF16), vblk, preferred_element_type=F32)
            car_scr[h] = jnp.broadcast_to(cs[:, 0:1], (blk, LANES))

    block(qi, True)

    def body(n, c):
        block(qi - 1 - n, False)
        return c

    lax.fori_loop(0, qi, body, 0)
    o_ref[0] = jnp.where(lane < HEAD_DIM, acc_scr[0], acc_scr[1])


def _attn_call(bias, qa, ka, va, u, *, blk):
    bsz, seq, d_att = qa.shape
    n_pairs = d_att // LANES
    kern = functools.partial(_attn_kernel, blk=blk)
    return pl.pallas_call(
        kern,
        grid_spec=pltpu.PrefetchScalarGridSpec(
            num_scalar_prefetch=1,
            grid=(bsz, n_pairs, seq // blk),
            in_specs=[
                pl.BlockSpec((1, blk, LANES), lambda b, p, i, s: (b, i, p)),
                pl.BlockSpec((1, seq, LANES), lambda b, p, i, s: (b, 0, p)),
                pl.BlockSpec((1, seq, LANES), lambda b, p, i, s: (b, 0, p)),
                pl.BlockSpec((blk, blk), lambda b, p, i, s: (0, 0)),
            ],
            out_specs=pl.BlockSpec((1, blk, LANES), lambda b, p, i, s: (b, i, p)),
            scratch_shapes=[
                pltpu.VMEM((2, blk, LANES), F32),
                pltpu.VMEM((2, blk, LANES), F32),
            ],
        ),
        out_shape=jax.ShapeDtypeStruct((bsz, seq, d_att), F32),
        compiler_params=_params(3),
        name="prompt_attention",
    )(bias, qa, ka, va, u)


def _same_head(r, c, t_new):
    assert t_new & (t_new - 1) == 0 and HEAD_DIM & (HEAD_DIM - 1) == 0
    return jnp.right_shift(r, t_new.bit_length() - 1) == jnp.right_shift(c, HEAD_DIM.bit_length() - 1)


def _sattn_kernel(pt_ref, q_ref, kn_ref, vn_ref, bias_ref, u_ref, *rest, n_pages_step, t_new, d_att):
    k_refs = rest[:n_pages_step]
    v_refs = rest[n_pages_step:2 * n_pages_step]
    o_ref = rest[2 * n_pages_step]
    qbd_scr, acc_scr, car_scr = rest[2 * n_pages_step + 1:]
    j = pl.program_id(1)
    nj = pl.num_programs(1)
    rows = N_HEADS * t_new
    u = u_ref[...]
    bias = bias_ref[...]

    def page(kp, vp, mask):
        kb = kp.astype(BF16)
        vb = vp.astype(BF16)
        z = lax.dot_general(qbd_scr[...], kb, (((1,), (1,)), ((), ())),
                            preferred_element_type=F32) + bias
        lf = _neg_softplus(z)
        if mask is not None:
            lf = jnp.where(mask, lf, 0.0)
        cs = jnp.dot(lf.astype(BF16), u, preferred_element_type=F32) + car_scr[...]
        w = jnp.exp(z + cs)
        if mask is not None:
            w = jnp.where(mask, w, 0.0)
        acc_scr[...] += jnp.dot(w.astype(BF16), vb, preferred_element_type=F32)
        car_scr[...] = jnp.broadcast_to(cs[:, 0:1], (rows, PAGE_SIZE))

    @pl.when(j == 0)
    def _():
        q = q_ref[0]
        qt = jnp.concatenate([q] * N_HEADS, axis=0)
        r = lax.broadcasted_iota(jnp.int32, (rows, d_att), 0)
        c = lax.broadcasted_iota(jnp.int32, (rows, d_att), 1)
        qbd_scr[...] = jnp.where(_same_head(r, c, t_new), qt, 0.0).astype(BF16)
        acc_scr[...] = jnp.zeros_like(acc_scr)
        car_scr[...] = jnp.zeros_like(car_scr)
        pad = jnp.zeros((PAGE_SIZE - t_new, d_att), F32)
        kn = jnp.concatenate([kn_ref[0], pad], axis=0)
        vn = jnp.concatenate([vn_ref[0], pad], axis=0)
        r2 = lax.broadcasted_iota(jnp.int32, (rows, PAGE_SIZE), 0)
        c2 = lax.broadcasted_iota(jnp.int32, (rows, PAGE_SIZE), 1)
        page(kn, vn, c2 < jnp.bitwise_and(r2, t_new - 1))

    for p in range(n_pages_step - 1, -1, -1):
        page(k_refs[p][0], v_refs[p][0], None)

    @pl.when(j == nj - 1)
    def _():
        acc = acc_scr[...]
        r = lax.broadcasted_iota(jnp.int32, (rows, d_att), 0)
        c = lax.broadcasted_iota(jnp.int32, (rows, d_att), 1)
        sel = jnp.where(_same_head(r, c, t_new), acc, 0.0)
        out = sel[0:t_new, :]
        for h in range(1, N_HEADS):
            out = out + sel[h * t_new:(h + 1) * t_new, :]
        o_ref[0] = out


def _sattn_call(page_table, q, k_new, v_new, bias_rows, u, cache_k, cache_v, *, layer, n_pool):
    bsz, t_new, d_att = q.shape
    n_pages = page_table.shape[1]
    pps = PAGES_PER_STEP
    n_steps = n_pages // pps
    rows = N_HEADS * t_new
    base = layer * n_pool

    def page_map(p):
        return lambda b, j, pt: (base + pt[b, (n_steps - 1 - j) * pps + p], 0, 0)

    per_b = lambda b, j, pt: (b, 0, 0)
    const2 = lambda b, j, pt: (0, 0)
    page_specs = [pl.BlockSpec((1, PAGE_SIZE, d_att), page_map(p)) for p in range(pps)]
    kern = functools.partial(_sattn_kernel, n_pages_step=pps, t_new=t_new, d_att=d_att)
    return pl.pallas_call(
        kern,
        grid_spec=pltpu.PrefetchScalarGridSpec(
            num_scalar_prefetch=1,
            grid=(bsz, n_steps),
            in_specs=[
                pl.BlockSpec((1, t_new, d_att), per_b),
                pl.BlockSpec((1, t_new, d_att), per_b),
                pl.BlockSpec((1, t_new, d_att), per_b),
                pl.BlockSpec((rows, PAGE_SIZE), const2),
                pl.BlockSpec((PAGE_SIZE, PAGE_SIZE), const2),
            ] + page_specs + page_specs,
            out_specs=pl.BlockSpec((1, t_new, d_att), per_b),
            scratch_shapes=[
                pltpu.VMEM((rows, d_att), BF16),
                pltpu.VMEM((rows, d_att), F32),
                pltpu.VMEM((rows, PAGE_SIZE), F32),
            ],
        ),
        out_shape=jax.ShapeDtypeStruct((bsz, t_new, d_att), F32),
        compiler_params=_params(2),
        name="sample_attention",
    )(page_table, q, k_new, v_new, bias_rows, u, *([cache_k] * pps), *([cache_v] * pps))


def _post_kernel(x_ref, yrec_ref, o_ref, sgb_ref, gate_ref, wout_ref, lng_ref, lnb_ref, out_ref,
                 *, d_rec, alpha):
    yatt = (o_ref[0] * sgb_ref[0]).astype(BF16)
    y = jnp.dot(yrec_ref[0], wout_ref[0:d_rec, :], preferred_element_type=F32)
    y = y + jnp.dot(yatt, wout_ref[d_rec:, :], preferred_element_type=F32)
    y = y * gate_ref[0]
    s = alpha * x_ref[0] + y
    out_ref[0] = _layer_norm(s) * lng_ref[...] + lnb_ref[...]


def _post_call(x, yrec, o, sgb, gate, wout, lng, lnb, *, rows, alpha):
    bsz, seq, d = x.shape
    d_rec = yrec.shape[2]
    d_att = o.shape[2]
    kern = functools.partial(_post_kernel, d_rec=d_rec, alpha=alpha)
    tile = lambda b, t: (b, t, 0)
    const2 = lambda b, t: (0, 0)
    return pl.pallas_call(
        kern,
        grid=(bsz, seq // rows),
        in_specs=[
            pl.BlockSpec((1, rows, d), tile),
            pl.BlockSpec((1, rows, d_rec), tile),
            pl.BlockSpec((1, rows, d_att), tile),
            pl.BlockSpec((1, rows, d_att), tile),
            pl.BlockSpec((1, 1, d), lambda b, t: (b, 0, 0)),
            pl.BlockSpec((d_rec + d_att, d), const2),
            pl.BlockSpec((1, d), const2),
            pl.BlockSpec((1, d), const2),
        ],
        out_specs=pl.BlockSpec((1, rows, d), tile),
        out_shape=jax.ShapeDtypeStruct((bsz, seq, d), F32),
        compiler_params=_params(2),
        name="post_mixer",
    )(x, yrec, o, sgb, gate, wout, lng, lnb)


def _suffix_ones(n):
    s = lax.broadcasted_iota(jnp.int32, (n, n), 0)
    j = lax.broadcasted_iota(jnp.int32, (n, n), 1)
    return (s >= j).astype(BF16)


def _block_diag(w):
    n, d, e = w.shape
    eye = jnp.eye(n, dtype=w.dtype)
    return (eye[:, None, :, None] * w[:, :, None, :]).reshape(n * d, n * e)


def kernel(x_prompt, x_sample, cache_k, cache_v, state_h, state_conv, page_table, c_prompt, c_sample,
           w_ada, b_ada, w_in, conv_w, conv_b, w_r, b_r, w_i, b_i, lam, sb_bias, w_out, ln_g, ln_b):
    depth = w_ada.shape[0]
    bp, seq, d = x_prompt.shape
    bs, t_new, _ = x_sample.shape
    d_rec = conv_w.shape[2]
    d_att = N_HEADS * HEAD_DIM
    n_pool = cache_k.shape[1]
    alpha = (2 * depth) ** 0.25

    n_c = bp + bs
    pad_rows = (-n_c) % SUBLANES
    c_all = jnp.concatenate([c_prompt, c_sample, jnp.zeros((pad_rows, d), F32)], axis=0)
    mod = _modulation(c_all, w_ada, b_ada)

    cache_k2 = cache_k.reshape(depth * n_pool, PAGE_SIZE, d_att)
    cache_v2 = cache_v.reshape(depth * n_pool, PAGE_SIZE, d_att)
    u_prompt = _suffix_ones(ATT_BLOCK)
    u_page = _suffix_ones(PAGE_SIZE)
    zeros_hist = jnp.zeros((bp, SUBLANES, d_rec), F32)
    zeros_h = jnp.zeros((bp, 1, d_rec), F32)
    hist_pad = jnp.zeros((depth, bs, SUBLANES - (CONV_W - 1), d_rec), F32)
    state_conv8 = jnp.concatenate([hist_pad, state_conv], axis=2)

    xp, xs = x_prompt, x_sample
    outs = [[] for _ in range(8)]
    for l in range(depth):
        shift, scale, gate = mod[l, :, :d], mod[l, :, d:2 * d], mod[l, :, 2 * d:]
        win = w_in[l].astype(BF16)
        wg = jnp.concatenate([_block_diag(w_r[l]), _block_diag(w_i[l])], axis=1).astype(BF16)
        bg = jnp.concatenate([b_r[l], b_i[l]])[None, :]
        wout = w_out[l].astype(BF16)
        weights = (win, conv_w[l], conv_b[l][None, :], wg, bg, lam[l][None, :])
        lng, lnb = ln_g[l][None, :], ln_b[l][None, :]

        sl = slice(0, bp)
        k, v, qa, ka, va, yrec, sgb, hl, cv = _pre_call(
            xp, shift[sl, None, :], scale[sl, None, :], *weights, zeros_hist, zeros_h,
            rows=PRE_ROWS, reset_first=True, q_dtype=BF16)
        o = _attn_call(sb_bias[l], qa, ka, va, u_prompt, blk=ATT_BLOCK)
        xp = _post_call(xp, yrec, o, sgb, gate[sl, None, :], wout, lng, lnb, rows=POST_ROWS, alpha=alpha)
        outs[0].append(k.reshape(bp, seq, N_HEADS, HEAD_DIM))
        outs[1].append(v.reshape(bp, seq, N_HEADS, HEAD_DIM))
        outs[2].append(hl[:, 0, :])
        outs[3].append(cv[:, SUBLANES - (CONV_W - 1):, :])

        sl = slice(bp, bp + bs)
        k, v, qa, _, _, yrec, sgb, hl, cv = _pre_call(
            xs, shift[sl, None, :], scale[sl, None, :], *weights, state_conv8[l], state_h[l][:, None, :],
            rows=t_new, reset_first=False, q_dtype=F32)
        bias_rows = jnp.broadcast_to(jnp.repeat(sb_bias[l], t_new)[:, None], (N_HEADS * t_new, PAGE_SIZE))
        o = _sattn_call(page_table, qa, k, v, bias_rows, u_page, cache_k2, cache_v2, layer=l, n_pool=n_pool)
        xs = _post_call(xs, yrec, o, sgb, gate[sl, None, :], wout, lng, lnb, rows=t_new, alpha=alpha)
        outs[4].append(k.reshape(bs, t_new, N_HEADS, HEAD_DIM))
        outs[5].append(v.reshape(bs, t_new, N_HEADS, HEAD_DIM))
        outs[6].append(hl[:, 0, :])
        outs[7].append(cv[:, SUBLANES - (CONV_W - 1):, :])

    stacked = [jnp.stack(o_) for o_ in outs]
    return (xp, xs, *stacked)
```

```python
import functools
import math

import jax
import jax.numpy as jnp
from jax import lax
from jax.experimental import pallas as pl
from jax.experimental.pallas import tpu as pltpu

F32 = jnp.float32
BF16 = jnp.bfloat16

N_HEADS = 8
HEAD_DIM = 64
N_REC_BLOCKS = 8
CONV_W = 4
RG_C = 8.0
PAGE_SIZE = 128
LN_EPS = 1e-5
LOG2E = math.log2(math.e)
Z2_MAX = 120.0

SUBLANES = 8
LANES = 128
VMEM_LIMIT = 56 * 1024 * 1024

PRE_ROWS = 256
POST_ROWS = 512
ATT_BLOCK = 256
ATT_PAIRS = 2
PAGES_PER_STEP = 16


def _sigmoid(x):
    return 1.0 / (1.0 + jnp.exp(-x))


def _layer_norm(x):
    mu = jnp.mean(x, axis=-1, keepdims=True)
    xc = x - mu
    var = jnp.mean(xc * xc, axis=-1, keepdims=True)
    return xc * lax.rsqrt(var + LN_EPS)


def _params(n_axes):
    return pltpu.CompilerParams(dimension_semantics=("arbitrary",) * n_axes,
                                vmem_limit_bytes=VMEM_LIMIT)


def _mod_kernel(c_ref, w_ref, b_ref, o_ref):
    c = c_ref[...]
    sc = c * _sigmoid(c)
    o_ref[0] = jnp.dot(sc, w_ref[0], precision=lax.Precision.HIGHEST,
                       preferred_element_type=F32) + b_ref[0]


def _modulation(c_all, w_ada, b_ada):
    depth, d, d3 = w_ada.shape
    rows = c_all.shape[0]
    n_col = d3 // d
    return pl.pallas_call(
        _mod_kernel,
        grid=(depth, n_col),
        in_specs=[
            pl.BlockSpec((rows, d), lambda l, n: (0, 0)),
            pl.BlockSpec((1, d, d), lambda l, n: (l, 0, n)),
            pl.BlockSpec((1, 1, d), lambda l, n: (l, 0, n)),
        ],
        out_specs=pl.BlockSpec((1, rows, d), lambda l, n: (l, 0, n)),
        out_shape=jax.ShapeDtypeStruct((depth, rows, d3), F32),
        compiler_params=_params(2),
        name="adaln_modulation",
    )(c_all, w_ada, b_ada.reshape(depth, 1, d3))


def _pre_kernel(x_ref, shift_ref, scale_ref, win_ref, cw_ref, cb_ref, wg_ref, bg_ref, lam_ref,
                hist0_ref, h0_ref,
                k_ref, v_ref, qa_ref, ka_ref, va_ref, yrec_ref, sgb_ref, hlast_ref, conv_ref,
                xp_scr, h_scr, a_scr, b_scr, sga_scr, *, rows, d_rec, d_att, reset_first, q_scale):
    ti = pl.program_id(1)

    @pl.when(ti == 0)
    def _():
        xp_scr[0:SUBLANES, :] = hist0_ref[0]
        h_scr[...] = jnp.broadcast_to(h0_ref[0], (SUBLANES, d_rec))

    x = x_ref[0]
    u = _layer_norm(x) * (1.0 + scale_ref[0]) + shift_ref[0]
    z = jnp.dot(u.astype(BF16), win_ref[...], preferred_element_type=F32)
    o1, o2 = d_rec, 2 * d_rec
    o3, o4, o5 = o2 + d_att, o2 + 2 * d_att, o2 + 3 * d_att
    xa = z[:, :o1]
    ga = z[:, o1:o2]
    q = z[:, o2:o3]
    k = z[:, o3:o4]
    v = z[:, o4:o5]
    gb = z[:, o5:]
    k_ref[0] = k
    v_ref[0] = v
    qa_ref[0] = (q * q_scale).astype(qa_ref.dtype)
    ka_ref[0] = k.astype(BF16)
    va_ref[0] = v.astype(BF16)
    sgb_ref[0] = gb * _sigmoid(gb)
    sga_scr[...] = ga * _sigmoid(ga)

    xp_scr[SUBLANES:SUBLANES + rows, :] = xa
    cw = cw_ref[...]
    xc = cb_ref[...] + cw[3:4, :] * xa
    for j in range(CONV_W - 1):
        off = SUBLANES - (CONV_W - 1) + j
        xc = xc + cw[j:j + 1, :] * xp_scr[off:off + rows, :]
    last_rows = xa[rows - SUBLANES:, :]
    xp_scr[0:SUBLANES, :] = last_rows
    conv_ref[0] = last_rows

    gates = jnp.dot(xc.astype(BF16), wg_ref[...], preferred_element_type=F32) + bg_ref[...]
    r = _sigmoid(gates[:, :d_rec])
    i = _sigmoid(gates[:, d_rec:])
    lam = lam_ref[...]
    c_lam = -RG_C * (jnp.maximum(-lam, 0.0) + jnp.log1p(jnp.exp(-jnp.abs(lam))))
    a = jnp.exp(c_lam * r)
    mult = jnp.sqrt(1.0 - a * a)
    if reset_first:
        row = lax.broadcasted_iota(jnp.int32, (rows, d_rec), 0)
        mult = jnp.where(jnp.logical_and(row == 0, ti == 0), 1.0, mult)
    a_scr[...] = a
    b_scr[...] = mult * (i * xc)

    srow = lax.broadcasted_iota(jnp.int32, (SUBLANES, d_rec), 0)

    def group(g, hprev):
        sl = pl.ds(pl.multiple_of(g * SUBLANES, SUBLANES), SUBLANES)
        ag = a_scr[sl, :]
        bg = b_scr[sl, :]
        for s in (1, 2, 4):
            keep = srow >= s
            a_sh = jnp.where(keep, pltpu.roll(ag, s, 0), 1.0)
            b_sh = jnp.where(keep, pltpu.roll(bg, s, 0), 0.0)
            bg = ag * b_sh + bg
            ag = ag * a_sh
        h = ag * hprev + bg
        b_scr[sl, :] = h
        return jnp.broadcast_to(h[SUBLANES - 1:SUBLANES, :], (SUBLANES, d_rec))

    hfin = lax.fori_loop(0, rows // SUBLANES, group, h_scr[...])
    h_scr[...] = hfin
    hlast_ref[0] = hfin[0:1, :]
    yrec_ref[0] = (b_scr[...] * sga_scr[...]).astype(BF16)


def _pre_call(x, shift, scale, win, cw, cb, wg, bg, lam, hist0, h0, *, rows, reset_first, q_dtype, q_scale):
    bsz, seq, d = x.shape
    d_in = win.shape[1]
    d_rec = cw.shape[1]
    d_att = (d_in - 2 * d_rec) // 4
    nt = seq // rows
    kern = functools.partial(_pre_kernel, rows=rows, d_rec=d_rec, d_att=d_att, reset_first=reset_first,
                             q_scale=q_scale)
    const2 = lambda b, t: (0, 0)
    per_b = lambda b, t: (b, 0, 0)
    tile = lambda b, t: (b, t, 0)
    out_shapes = (
        jax.ShapeDtypeStruct((bsz, seq, d_att), F32),
        jax.ShapeDtypeStruct((bsz, seq, d_att), F32),
        jax.ShapeDtypeStruct((bsz, seq, d_att), q_dtype),
        jax.ShapeDtypeStruct((bsz, seq, d_att), BF16),
        jax.ShapeDtypeStruct((bsz, seq, d_att), BF16),
        jax.ShapeDtypeStruct((bsz, seq, d_rec), BF16),
        jax.ShapeDtypeStruct((bsz, seq, d_att), F32),
        jax.ShapeDtypeStruct((bsz, 1, d_rec), F32),
        jax.ShapeDtypeStruct((bsz, SUBLANES, d_rec), F32),
    )
    out_specs = (
        pl.BlockSpec((1, rows, d_att), tile),
        pl.BlockSpec((1, rows, d_att), tile),
        pl.BlockSpec((1, rows, d_att), tile),
        pl.BlockSpec((1, rows, d_att), tile),
        pl.BlockSpec((1, rows, d_att), tile),
        pl.BlockSpec((1, rows, d_rec), tile),
        pl.BlockSpec((1, rows, d_att), tile),
        pl.BlockSpec((1, 1, d_rec), per_b),
        pl.BlockSpec((1, SUBLANES, d_rec), per_b),
    )
    in_specs = [
        pl.BlockSpec((1, rows, d), tile),
        pl.BlockSpec((1, 1, d), per_b),
        pl.BlockSpec((1, 1, d), per_b),
        pl.BlockSpec((d, d_in), const2),
        pl.BlockSpec((CONV_W, d_rec), const2),
        pl.BlockSpec((1, d_rec), const2),
        pl.BlockSpec((d_rec, 2 * d_rec), const2),
        pl.BlockSpec((1, 2 * d_rec), const2),
        pl.BlockSpec((1, d_rec), const2),
        pl.BlockSpec((1, SUBLANES, d_rec), per_b),
        pl.BlockSpec((1, 1, d_rec), per_b),
    ]
    return pl.pallas_call(
        kern,
        grid=(bsz, nt),
        in_specs=in_specs,
        out_specs=out_specs,
        out_shape=out_shapes,
        scratch_shapes=[
            pltpu.VMEM((SUBLANES + rows, d_rec), F32),
            pltpu.VMEM((SUBLANES, d_rec), F32),
            pltpu.VMEM((rows, d_rec), F32),
            pltpu.VMEM((rows, d_rec), F32),
            pltpu.VMEM((rows, d_rec), F32),
        ],
        compiler_params=_params(2),
        name="pre_mixer",
    )(x, shift, scale, win, cw, cb, wg, bg, lam, hist0, h0)


def _attn_kernel(q_ref, k_ref, v_ref, bcol_ref, u_ref, o_ref, acc_scr, car_scr,
                 z0_scr, z1_scr, sp0_scr, sp1_scr, w0_scr, w1_scr, *, blk, pairs):
    qi = pl.program_id(2)
    heads = 2 * pairs
    z_scr, sp_scr, w_scr = (z0_scr, z1_scr), (sp0_scr, sp1_scr), (w0_scr, w1_scr)
    lane = lax.broadcasted_iota(jnp.int32, (blk, LANES), 1)
    row = lax.broadcasted_iota(jnp.int32, (blk, blk), 0)
    col = lax.broadcasted_iota(jnp.int32, (blk, blk), 1)
    causal = col < row
    u = u_ref[...]
    dn_nt = (((1,), (1,)), ((), ()))

    q_aug = []
    for h in range(heads):
        p, s = divmod(h, 2)
        qp = q_ref[0, :, p * LANES:(p + 1) * LANES]
        in_head = (lane >= HEAD_DIM) if s else (lane < HEAD_DIM)
        qh = jnp.where(in_head, qp, jnp.zeros_like(qp))
        sel = jnp.where((lane == 2 * s) | (lane == 2 * s + 1), 1.0, 0.0).astype(BF16)
        q_aug.append(jnp.concatenate([qh, sel], axis=1))

    acc_scr[...] = jnp.zeros_like(acc_scr)
    car_scr[...] = jnp.zeros_like(car_scr)

    def rows_of(kb):
        return pl.ds(pl.multiple_of(kb * blk, blk), blk)

    def stage1(kb, slot, masked):
        for h in range(heads):
            p = h // 2
            k_aug = jnp.concatenate([k_ref[0, rows_of(kb), p * LANES:(p + 1) * LANES], bcol_ref[p]], axis=1)
            z2 = lax.dot_general(q_aug[h], k_aug, dn_nt, preferred_element_type=F32)
            z2 = jnp.minimum(z2, Z2_MAX)
            sp = jnp.log(1.0 + jnp.exp2(z2)) * LOG2E
            if masked:
                sp = jnp.where(causal, sp, 0.0)
            z_scr[slot][h] = z2
            sp_scr[slot][h] = sp.astype(BF16)

    def stage2(slot, masked, keep=None):
        for h in range(heads):
            cs = jnp.dot(sp_scr[slot][h], u, preferred_element_type=F32)
            cs = cs + jnp.concatenate([car_scr[h]] * (blk // LANES), axis=1)
            w = jnp.exp2(cs + z_scr[slot][h])
            if masked:
                w = jnp.where(causal, w, 0.0)
            if keep is not None:
                w = jnp.where(keep, w, 0.0)
            car_scr[h] = jnp.broadcast_to(cs[:, 0:1], (blk, LANES))
            w_scr[slot][h] = w.astype(BF16)

    def stage3(slot, kb):
        for h in range(heads):
            p = h // 2
            acc_scr[h] += jnp.dot(w_scr[slot][h], v_ref[0, rows_of(kb), p * LANES:(p + 1) * LANES],
                                  preferred_element_type=F32)

    def step(t, slot):
        kb = qi - t
        stage3(slot, kb + 2)
        stage2(1 - slot, False)
        stage1(kb, slot, False)

    stage1(qi, 0, True)
    stage2(0, True)
    stage1(jnp.maximum(qi - 1, 0), 1, False)

    def two_steps(n, c):
        t = 2 + 2 * n
        step(t, 0)
        step(t + 1, 1)
        return c

    n_pairs_of_steps = jnp.maximum(qi - 1, 0) // 2
    lax.fori_loop(0, n_pairs_of_steps, two_steps, 0)
    odd = jnp.logical_and(qi >= 2, qi % 2 == 0)

    @pl.when(odd)
    def _():
        step(qi, 0)
        stage3(1, 1)
        stage2(0, False)
        stage3(0, 0)

    @pl.when(jnp.logical_not(odd))
    def _():
        stage3(0, jnp.minimum(qi, 1))
        stage2(1, False, keep=qi >= 1)
        stage3(1, 0)

    for p in range(pairs):
        o_ref[0, :, p * LANES:(p + 1) * LANES] = jnp.where(lane < HEAD_DIM, acc_scr[2 * p], acc_scr[2 * p + 1])


def _attn_call(qa, ka, va, bcol, u, *, blk, pairs):
    bsz, seq, d_att = qa.shape
    width = pairs * LANES
    n_groups = d_att // width
    kern = functools.partial(_attn_kernel, blk=blk, pairs=pairs)
    return pl.pallas_call(
        kern,
        grid=(bsz, n_groups, seq // blk),
        in_specs=[
            pl.BlockSpec((1, blk, width), lambda b, g, i: (b, i, g)),
            pl.BlockSpec((1, seq, width), lambda b, g, i: (b, 0, g)),
            pl.BlockSpec((1, seq, width), lambda b, g, i: (b, 0, g)),
            pl.BlockSpec((pairs, blk, LANES), lambda b, g, i: (g, 0, 0)),
            pl.BlockSpec((blk, blk), lambda b, g, i: (0, 0)),
        ],
        out_specs=pl.BlockSpec((1, blk, width), lambda b, g, i: (b, i, g)),
        out_shape=jax.ShapeDtypeStruct((bsz, seq, d_att), F32),
        scratch_shapes=[
            pltpu.VMEM((2 * pairs, blk, LANES), F32),
            pltpu.VMEM((2 * pairs, blk, LANES), F32),
            pltpu.VMEM((2 * pairs, blk, blk), F32),
            pltpu.VMEM((2 * pairs, blk, blk), F32),
            pltpu.VMEM((2 * pairs, blk, blk), BF16),
            pltpu.VMEM((2 * pairs, blk, blk), BF16),
            pltpu.VMEM((2 * pairs, blk, blk), BF16),
            pltpu.VMEM((2 * pairs, blk, blk), BF16),
        ],
        compiler_params=_params(3),
        name="prompt_attention",
    )(qa, ka, va, bcol, u)


def _bias_columns(sb_bias_l, blk):
    nb = sb_bias_l.astype(F32) * LOG2E
    hi = nb.astype(BF16)
    lo = (nb - hi.astype(F32)).astype(BF16)
    cols = jnp.stack([hi[0::2], lo[0::2], hi[1::2], lo[1::2]], axis=1)
    cols = jnp.pad(cols, ((0, 0), (0, LANES - 4)))
    return jnp.broadcast_to(cols[:, None, :], (cols.shape[0], blk, LANES))


def _same_head(r, c, t_new):
    assert t_new & (t_new - 1) == 0 and HEAD_DIM & (HEAD_DIM - 1) == 0
    return jnp.right_shift(r, t_new.bit_length() - 1) == jnp.right_shift(c, HEAD_DIM.bit_length() - 1)


def _sattn_kernel(pt_ref, q_ref, kn_ref, vn_ref, bias_ref, un_ref, up_ref, *rest, n_pages_step, t_new, d_att):
    k_refs = rest[:n_pages_step]
    v_refs = rest[n_pages_step:2 * n_pages_step]
    o_ref = rest[2 * n_pages_step]
    qbd_scr, qm_scr, accn_scr, acco_scr, car_scr, z_scr, sp_scr, w_scr = rest[2 * n_pages_step + 1:]
    j = pl.program_id(1)
    nj = pl.num_programs(1)
    rows = N_HEADS * t_new
    n_tiles = PAGE_SIZE * N_HEADS // LANES
    bias = bias_ref[...]
    dn_nt = (((1,), (1,)), ((), ()))
    r1 = lax.broadcasted_iota(jnp.int32, (rows, LANES), 0)
    c1 = lax.broadcasted_iota(jnp.int32, (rows, LANES), 1)
    log_t = t_new.bit_length() - 1
    same_head_tile = jnp.right_shift(r1, log_t) == jnp.bitwise_and(c1, N_HEADS - 1)

    def softplus2(z2):
        return jnp.log(1.0 + jnp.exp2(z2)) * LOG2E

    @pl.when(j == 0)
    def _():
        q = q_ref[0]
        qt = jnp.concatenate([q] * N_HEADS, axis=0)
        r = lax.broadcasted_iota(jnp.int32, (rows, d_att), 0)
        c = lax.broadcasted_iota(jnp.int32, (rows, d_att), 1)
        qbd = jnp.where(_same_head(r, c, t_new), qt, 0.0).astype(BF16)
        qbd_scr[...] = qbd
        qm_scr[...] = jnp.concatenate(
            [q[:, h * HEAD_DIM:(h + 1) * HEAD_DIM] for h in range(N_HEADS)], axis=0).astype(BF16)
        acco_scr[...] = jnp.zeros_like(acco_scr)
        pad = jnp.zeros((PAGE_SIZE - t_new, d_att), F32)
        kn = jnp.concatenate([kn_ref[0], pad], axis=0).astype(BF16)
        vn = jnp.concatenate([vn_ref[0], pad], axis=0).astype(BF16)
        visible = c1 < jnp.bitwise_and(r1, t_new - 1)
        z2 = jnp.minimum(lax.dot_general(qbd, kn, dn_nt, preferred_element_type=F32) + bias, Z2_MAX)
        sp = jnp.where(visible, softplus2(z2), 0.0)
        cs = jnp.dot(sp.astype(BF16), un_ref[...], preferred_element_type=F32)
        w = jnp.where(visible, jnp.exp2(cs + z2), 0.0)
        accn_scr[...] = jnp.dot(w.astype(BF16), vn, preferred_element_type=F32)
        car_scr[...] = jnp.broadcast_to(cs[:, 0:1], (rows, LANES))

    def scores(p):
        kb = k_refs[p][...].reshape(PAGE_SIZE * N_HEADS, HEAD_DIM).astype(BF16)
        z2 = lax.dot_general(qm, kb, dn_nt, preferred_element_type=F32)
        for i in range(n_tiles):
            z2i = jnp.minimum(z2[:, i * LANES:(i + 1) * LANES] + bias, Z2_MAX)
            z_scr[p, :, i * LANES:(i + 1) * LANES] = z2i
            sp_scr[p, i * rows:(i + 1) * rows, :] = softplus2(z2i).astype(BF16)

    def weights(p, run):
        csl = jnp.dot(sp_scr[p], up, preferred_element_type=F32)
        for i in range(n_tiles - 1, -1, -1):
            blk_rows = slice(i * rows, (i + 1) * rows)
            cs = csl[blk_rows, 0:LANES] + run
            run = run + csl[blk_rows, LANES:2 * LANES]
            z2i = z_scr[p, :, i * LANES:(i + 1) * LANES]
            w = jnp.where(same_head_tile, jnp.exp2(cs + z2i), 0.0)
            w_scr[p, :, i * LANES:(i + 1) * LANES] = w.astype(BF16)
        return run

    def values(p):
        vb = v_refs[p][...].reshape(PAGE_SIZE * N_HEADS, HEAD_DIM).astype(BF16)
        return jnp.dot(w_scr[p], vb, preferred_element_type=F32)

    qm = qm_scr[...]
    up = up_ref[...]
    run = car_scr[...]
    acc = acco_scr[...]
    for s in range(n_pages_step + 2):
        p = n_pages_step - 1 - s
        if p + 2 < n_pages_step:
            acc = acc + values(p + 2)
        if 0 <= p + 1 < n_pages_step:
            run = weights(p + 1, run)
        if p >= 0:
            scores(p)
    car_scr[...] = run
    acco_scr[...] = acc

    @pl.when(j == nj - 1)
    def _():
        acc = accn_scr[...]
        r = lax.broadcasted_iota(jnp.int32, (rows, d_att), 0)
        c = lax.broadcasted_iota(jnp.int32, (rows, d_att), 1)
        sel = jnp.where(_same_head(r, c, t_new), acc, 0.0)
        out = sel[0:t_new, :]
        for h in range(1, N_HEADS):
            out = out + sel[h * t_new:(h + 1) * t_new, :]
        acco = acco_scr[...]
        old = jnp.concatenate([acco[h * t_new:(h + 1) * t_new, :] for h in range(N_HEADS)], axis=1)
        o_ref[0] = out + old


def _page_suffix_matrix():
    i = lax.broadcasted_iota(jnp.int32, (LANES, LANES), 0)
    c = lax.broadcasted_iota(jnp.int32, (LANES, LANES), 1)
    same = jnp.bitwise_and(i, N_HEADS - 1) == jnp.bitwise_and(c, N_HEADS - 1)
    later = i >= c
    left = jnp.where(same & later, -1.0, 0.0)
    right = jnp.where(same, -1.0, 0.0)
    return jnp.concatenate([left, right], axis=1).astype(BF16)


def _sattn_call(page_table, q, k_new, v_new, bias_rows, cache_k, cache_v, *, layer):
    bsz, t_new, d_att = q.shape
    n_pages = page_table.shape[1]
    pps = PAGES_PER_STEP
    n_steps = n_pages // pps
    rows = N_HEADS * t_new

    def page_map(p):
        return lambda b, j, pt: (layer, pt[b, (n_steps - 1 - j) * pps + p], 0, 0, 0)

    per_b = lambda b, j, pt: (b, 0, 0)
    const2 = lambda b, j, pt: (0, 0)
    page_specs = [pl.BlockSpec((None, None, PAGE_SIZE, N_HEADS, HEAD_DIM), page_map(p)) for p in range(pps)]
    kern = functools.partial(_sattn_kernel, n_pages_step=pps, t_new=t_new, d_att=d_att)
    return pl.pallas_call(
        kern,
        grid_spec=pltpu.PrefetchScalarGridSpec(
            num_scalar_prefetch=1,
            grid=(bsz, n_steps),
            in_specs=[
                pl.BlockSpec((1, t_new, d_att), per_b),
                pl.BlockSpec((1, t_new, d_att), per_b),
                pl.BlockSpec((1, t_new, d_att), per_b),
                pl.BlockSpec((rows, LANES), const2),
                pl.BlockSpec((PAGE_SIZE, PAGE_SIZE), const2),
                pl.BlockSpec((LANES, 2 * LANES), const2),
            ] + page_specs + page_specs,
            out_specs=pl.BlockSpec((1, t_new, d_att), per_b),
            scratch_shapes=[
                pltpu.VMEM((rows, d_att), BF16),
                pltpu.VMEM((rows, HEAD_DIM), BF16),
                pltpu.VMEM((rows, d_att), F32),
                pltpu.VMEM((rows, HEAD_DIM), F32),
                pltpu.VMEM((rows, LANES), F32),
                pltpu.VMEM((pps, rows, PAGE_SIZE * N_HEADS), F32),
                pltpu.VMEM((pps, PAGE_SIZE * N_HEADS // LANES * rows, LANES), BF16),
                pltpu.VMEM((pps, rows, PAGE_SIZE * N_HEADS), BF16),
            ],
        ),
        out_shape=jax.ShapeDtypeStruct((bsz, t_new, d_att), F32),
        compiler_params=_params(2),
        name="sample_attention",
    )(page_table, q, k_new, v_new, bias_rows, -_suffix_ones(PAGE_SIZE), _page_suffix_matrix(),
      *([cache_k] * pps), *([cache_v] * pps))


def _post_kernel(x_ref, yrec_ref, o_ref, sgb_ref, gate_ref, wout_ref, lng_ref, lnb_ref, out_ref,
                 *, d_rec, alpha):
    yatt = (o_ref[0] * sgb_ref[0]).astype(BF16)
    y = jnp.dot(yrec_ref[0], wout_ref[0:d_rec, :], preferred_element_type=F32)
    y = y + jnp.dot(yatt, wout_ref[d_rec:, :], preferred_element_type=F32)
    y = y * gate_ref[0]
    s = alpha * x_ref[0] + y
    out_ref[0] = _layer_norm(s) * lng_ref[...] + lnb_ref[...]


def _post_call(x, yrec, o, sgb, gate, wout, lng, lnb, *, rows, alpha):
    bsz, seq, d = x.shape
    d_rec = yrec.shape[2]
    d_att = o.shape[2]
    kern = functools.partial(_post_kernel, d_rec=d_rec, alpha=alpha)
    tile = lambda b, t: (b, t, 0)
    const2 = lambda b, t: (0, 0)
    return pl.pallas_call(
        kern,
        grid=(bsz, seq // rows),
        in_specs=[
            pl.BlockSpec((1, rows, d), tile),
            pl.BlockSpec((1, rows, d_rec), tile),
            pl.BlockSpec((1, rows, d_att), tile),
            pl.BlockSpec((1, rows, d_att), tile),
            pl.BlockSpec((1, 1, d), lambda b, t: (b, 0, 0)),
            pl.BlockSpec((d_rec + d_att, d), const2),
            pl.BlockSpec((1, d), const2),
            pl.BlockSpec((1, d), const2),
        ],
        out_specs=pl.BlockSpec((1, rows, d), tile),
        out_shape=jax.ShapeDtypeStruct((bsz, seq, d), F32),
        compiler_params=_params(2),
        name="post_mixer",
    )(x, yrec, o, sgb, gate, wout, lng, lnb)


def _suffix_ones(n):
    s = lax.broadcasted_iota(jnp.int32, (n, n), 0)
    j = lax.broadcasted_iota(jnp.int32, (n, n), 1)
    return (s >= j).astype(BF16)


def _block_diag(w):
    n, d, e = w.shape
    eye = jnp.eye(n, dtype=w.dtype)
    return (eye[:, None, :, None] * w[:, :, None, :]).reshape(n * d, n * e)


def kernel(x_prompt, x_sample, cache_k, cache_v, state_h, state_conv, page_table, c_prompt, c_sample,
           w_ada, b_ada, w_in, conv_w, conv_b, w_r, b_r, w_i, b_i, lam, sb_bias, w_out, ln_g, ln_b):
    depth = w_ada.shape[0]
    bp, seq, d = x_prompt.shape
    bs, t_new, _ = x_sample.shape
    d_rec = conv_w.shape[2]
    d_att = N_HEADS * HEAD_DIM
    alpha = (2 * depth) ** 0.25

    n_c = bp + bs
    pad_rows = (-n_c) % SUBLANES
    c_all = jnp.concatenate([c_prompt, c_sample, jnp.zeros((pad_rows, d), F32)], axis=0)
    mod = _modulation(c_all, w_ada, b_ada)

    u_prompt = -_suffix_ones(ATT_BLOCK)
    zeros_hist = jnp.zeros((bp, SUBLANES, d_rec), F32)
    zeros_h = jnp.zeros((bp, 1, d_rec), F32)
    hist_pad = jnp.zeros((depth, bs, SUBLANES - (CONV_W - 1), d_rec), F32)
    state_conv8 = jnp.concatenate([hist_pad, state_conv], axis=2)

    xp, xs = x_prompt, x_sample
    outs = [[] for _ in range(8)]
    for l in range(depth):
        shift, scale, gate = mod[l, :, :d], mod[l, :, d:2 * d], mod[l, :, 2 * d:]
        win = w_in[l].astype(BF16)
        wg = jnp.concatenate([_block_diag(w_r[l]), _block_diag(w_i[l])], axis=1).astype(BF16)
        bg = jnp.concatenate([b_r[l], b_i[l]])[None, :]
        wout = w_out[l].astype(BF16)
        weights = (win, conv_w[l], conv_b[l][None, :], wg, bg, lam[l][None, :])
        lng, lnb = ln_g[l][None, :], ln_b[l][None, :]

        sl = slice(0, bp)
        k, v, qa, ka, va, yrec, sgb, hl, cv = _pre_call(
            xp, shift[sl, None, :], scale[sl, None, :], *weights, zeros_hist, zeros_h,
            rows=PRE_ROWS, reset_first=True, q_dtype=BF16, q_scale=LOG2E / math.sqrt(HEAD_DIM))
        o = _attn_call(qa, ka, va, _bias_columns(sb_bias[l], ATT_BLOCK), u_prompt, blk=ATT_BLOCK, pairs=ATT_PAIRS)
        xp = _post_call(xp, yrec, o, sgb, gate[sl, None, :], wout, lng, lnb, rows=POST_ROWS, alpha=alpha)
        outs[0].append(k.reshape(bp, seq, N_HEADS, HEAD_DIM))
        outs[1].append(v.reshape(bp, seq, N_HEADS, HEAD_DIM))
        outs[2].append(hl[:, 0, :])
        outs[3].append(cv[:, SUBLANES - (CONV_W - 1):, :])

        sl = slice(bp, bp + bs)
        k, v, qa, _, _, yrec, sgb, hl, cv = _pre_call(
            xs, shift[sl, None, :], scale[sl, None, :], *weights, state_conv8[l], state_h[l][:, None, :],
            rows=t_new, reset_first=False, q_dtype=F32, q_scale=LOG2E / math.sqrt(HEAD_DIM))
        bias_rows = jnp.broadcast_to(jnp.repeat(sb_bias[l] * LOG2E, t_new)[:, None], (N_HEADS * t_new, LANES))
        o = _sattn_call(page_table, qa, k, v, bias_rows, cache_k, cache_v, layer=l)
        xs = _post_call(xs, yrec, o, sgb, gate[sl, None, :], wout, lng, lnb, rows=t_new, alpha=alpha)
        outs[4].append(k.reshape(bs, t_new, N_HEADS, HEAD_DIM))
        outs[5].append(v.reshape(bs, t_new, N_HEADS, HEAD_DIM))
        outs[6].append(hl[:, 0, :])
        outs[7].append(cv[:, SUBLANES - (CONV_W - 1):, :])

    stacked = [jnp.stack(o_) for o_ in outs]
    return (xp, xs, *stacked)
```

```python
import functools
import math

import jax
import jax.numpy as jnp
from jax import lax
from jax.experimental import pallas as pl
from jax.experimental.pallas import tpu as pltpu

F32 = jnp.float32
BF16 = jnp.bfloat16

N_HEADS = 8
HEAD_DIM = 64
N_REC_BLOCKS = 8
CONV_W = 4
RG_C = 8.0
PAGE_SIZE = 128
LN_EPS = 1e-5
LOG2E = math.log2(math.e)
Z2_MAX = 120.0

SUBLANES = 8
LANES = 128
VMEM_LIMIT = 56 * 1024 * 1024

PRE_ROWS = 256
POST_ROWS = 512
ATT_BLOCK = 256
ATT_PAIRS = 2
PAGES_PER_STEP = 16


def _sigmoid(x):
    return 1.0 / (1.0 + jnp.exp(-x))


def _layer_norm(x):
    mu = jnp.mean(x, axis=-1, keepdims=True)
    xc = x - mu
    var = jnp.mean(xc * xc, axis=-1, keepdims=True)
    return xc * lax.rsqrt(var + LN_EPS)


def _params(n_axes):
    return pltpu.CompilerParams(dimension_semantics=("arbitrary",) * n_axes,
                                vmem_limit_bytes=VMEM_LIMIT)


def _mod_kernel(c_ref, w_ref, b_ref, o_ref):
    c = c_ref[...]
    sc = c * _sigmoid(c)
    o_ref[0] = jnp.dot(sc, w_ref[0], precision=lax.Precision.HIGHEST,
                       preferred_element_type=F32) + b_ref[0]


def _modulation(c_all, w_ada, b_ada):
    depth, d, d3 = w_ada.shape
    rows = c_all.shape[0]
    n_col = d3 // d
    return pl.pallas_call(
        _mod_kernel,
        grid=(depth, n_col),
        in_specs=[
            pl.BlockSpec((rows, d), lambda l, n: (0, 0)),
            pl.BlockSpec((1, d, d), lambda l, n: (l, 0, n)),
            pl.BlockSpec((1, 1, d), lambda l, n: (l, 0, n)),
        ],
        out_specs=pl.BlockSpec((1, rows, d), lambda l, n: (l, 0, n)),
        out_shape=jax.ShapeDtypeStruct((depth, rows, d3), F32),
        compiler_params=_params(2),
        name="adaln_modulation",
    )(c_all, w_ada, b_ada.reshape(depth, 1, d3))


def _pre_kernel(x_ref, shift_ref, scale_ref, win_ref, cw_ref, cb_ref, wg_ref, bg_ref, lam_ref,
                hist0_ref, h0_ref,
                k_ref, v_ref, qa_ref, ka_ref, va_ref, yrec_ref, sgb_ref, hlast_ref, conv_ref,
                xp_scr, h_scr, a_scr, b_scr, sga_scr, *, rows, d_rec, d_att, reset_first, q_scale):
    ti = pl.program_id(1)

    @pl.when(ti == 0)
    def _():
        xp_scr[0:SUBLANES, :] = hist0_ref[0]
        h_scr[...] = jnp.broadcast_to(h0_ref[0], (SUBLANES, d_rec))

    x = x_ref[0]
    u = _layer_norm(x) * (1.0 + scale_ref[0]) + shift_ref[0]
    z = jnp.dot(u.astype(BF16), win_ref[...], preferred_element_type=F32)
    o1, o2 = d_rec, 2 * d_rec
    o3, o4, o5 = o2 + d_att, o2 + 2 * d_att, o2 + 3 * d_att
    xa = z[:, :o1]
    ga = z[:, o1:o2]
    q = z[:, o2:o3]
    k = z[:, o3:o4]
    v = z[:, o4:o5]
    gb = z[:, o5:]
    k_ref[0] = k
    v_ref[0] = v
    qa_ref[0] = (q * q_scale).astype(qa_ref.dtype)
    ka_ref[0] = k.astype(BF16)
    va_ref[0] = v.astype(BF16)
    sgb_ref[0] = gb * _sigmoid(gb)
    sga_scr[...] = ga * _sigmoid(ga)

    xp_scr[SUBLANES:SUBLANES + rows, :] = xa
    cw = cw_ref[...]
    xc = cb_ref[...] + cw[3:4, :] * xa
    for j in range(CONV_W - 1):
        off = SUBLANES - (CONV_W - 1) + j
        xc = xc + cw[j:j + 1, :] * xp_scr[off:off + rows, :]
    last_rows = xa[rows - SUBLANES:, :]
    xp_scr[0:SUBLANES, :] = last_rows
    conv_ref[0] = last_rows

    gates = jnp.dot(xc.astype(BF16), wg_ref[...], preferred_element_type=F32) + bg_ref[...]
    r = _sigmoid(gates[:, :d_rec])
    i = _sigmoid(gates[:, d_rec:])
    lam = lam_ref[...]
    c_lam = -RG_C * (jnp.maximum(-lam, 0.0) + jnp.log1p(jnp.exp(-jnp.abs(lam))))
    a = jnp.exp(c_lam * r)
    mult = jnp.sqrt(1.0 - a * a)
    if reset_first:
        row = lax.broadcasted_iota(jnp.int32, (rows, d_rec), 0)
        mult = jnp.where(jnp.logical_and(row == 0, ti == 0), 1.0, mult)
    a_scr[...] = a
    b_scr[...] = mult * (i * xc)

    srow = lax.broadcasted_iota(jnp.int32, (SUBLANES, d_rec), 0)

    def group(g, hprev):
        sl = pl.ds(pl.multiple_of(g * SUBLANES, SUBLANES), SUBLANES)
        ag = a_scr[sl, :]
        bg = b_scr[sl, :]
        for s in (1, 2, 4):
            keep = srow >= s
            a_sh = jnp.where(keep, pltpu.roll(ag, s, 0), 1.0)
            b_sh = jnp.where(keep, pltpu.roll(bg, s, 0), 0.0)
            bg = ag * b_sh + bg
            ag = ag * a_sh
        h = ag * hprev + bg
        b_scr[sl, :] = h
        return jnp.broadcast_to(h[SUBLANES - 1:SUBLANES, :], (SUBLANES, d_rec))

    hfin = lax.fori_loop(0, rows // SUBLANES, group, h_scr[...])
    h_scr[...] = hfin
    hlast_ref[0] = hfin[0:1, :]
    yrec_ref[0] = (b_scr[...] * sga_scr[...]).astype(BF16)


def _pre_call(x, shift, scale, win, cw, cb, wg, bg, lam, hist0, h0, *, rows, reset_first, q_dtype, q_scale):
    bsz, seq, d = x.shape
    d_in = win.shape[1]
    d_rec = cw.shape[1]
    d_att = (d_in - 2 * d_rec) // 4
    nt = seq // rows
    kern = functools.partial(_pre_kernel, rows=rows, d_rec=d_rec, d_att=d_att, reset_first=reset_first,
                             q_scale=q_scale)
    const2 = lambda b, t: (0, 0)
    per_b = lambda b, t: (b, 0, 0)
    tile = lambda b, t: (b, t, 0)
    out_shapes = (
        jax.ShapeDtypeStruct((bsz, seq, d_att), F32),
        jax.ShapeDtypeStruct((bsz, seq, d_att), F32),
        jax.ShapeDtypeStruct((bsz, seq, d_att), q_dtype),
        jax.ShapeDtypeStruct((bsz, seq, d_att), BF16),
        jax.ShapeDtypeStruct((bsz, seq, d_att), BF16),
        jax.ShapeDtypeStruct((bsz, seq, d_rec), BF16),
        jax.ShapeDtypeStruct((bsz, seq, d_att), F32),
        jax.ShapeDtypeStruct((bsz, 1, d_rec), F32),
        jax.ShapeDtypeStruct((bsz, SUBLANES, d_rec), F32),
    )
    out_specs = (
        pl.BlockSpec((1, rows, d_att), tile),
        pl.BlockSpec((1, rows, d_att), tile),
        pl.BlockSpec((1, rows, d_att), tile),
        pl.BlockSpec((1, rows, d_att), tile),
        pl.BlockSpec((1, rows, d_att), tile),
        pl.BlockSpec((1, rows, d_rec), tile),
        pl.BlockSpec((1, rows, d_att), tile),
        pl.BlockSpec((1, 1, d_rec), per_b),
        pl.BlockSpec((1, SUBLANES, d_rec), per_b),
    )
    in_specs = [
        pl.BlockSpec((1, rows, d), tile),
        pl.BlockSpec((1, 1, d), per_b),
        pl.BlockSpec((1, 1, d), per_b),
        pl.BlockSpec((d, d_in), const2),
        pl.BlockSpec((CONV_W, d_rec), const2),
        pl.BlockSpec((1, d_rec), const2),
        pl.BlockSpec((d_rec, 2 * d_rec), const2),
        pl.BlockSpec((1, 2 * d_rec), const2),
        pl.BlockSpec((1, d_rec), const2),
        pl.BlockSpec((1, SUBLANES, d_rec), per_b),
        pl.BlockSpec((1, 1, d_rec), per_b),
    ]
    return pl.pallas_call(
        kern,
        grid=(bsz, nt),
        in_specs=in_specs,
        out_specs=out_specs,
        out_shape=out_shapes,
        scratch_shapes=[
            pltpu.VMEM((SUBLANES + rows, d_rec), F32),
            pltpu.VMEM((SUBLANES, d_rec), F32),
            pltpu.VMEM((rows, d_rec), F32),
            pltpu.VMEM((rows, d_rec), F32),
            pltpu.VMEM((rows, d_rec), F32),
        ],
        compiler_params=_params(2),
        name="pre_mixer",
    )(x, shift, scale, win, cw, cb, wg, bg, lam, hist0, h0)


def _attn_kernel(q_ref, k_ref, v_ref, bcol_ref, u_ref, o_ref, acc_scr, car_scr,
                 z0_scr, z1_scr, sp0_scr, sp1_scr, w0_scr, w1_scr, *, blk, pairs):
    qi = pl.program_id(2)
    heads = 2 * pairs
    z_scr, sp_scr, w_scr = (z0_scr, z1_scr), (sp0_scr, sp1_scr), (w0_scr, w1_scr)
    lane = lax.broadcasted_iota(jnp.int32, (blk, LANES), 1)
    row = lax.broadcasted_iota(jnp.int32, (blk, blk), 0)
    col = lax.broadcasted_iota(jnp.int32, (blk, blk), 1)
    causal = col < row
    u = u_ref[...]
    dn_nt = (((1,), (1,)), ((), ()))

    q_aug = []
    for h in range(heads):
        p, s = divmod(h, 2)
        qp = q_ref[0, :, p * LANES:(p + 1) * LANES]
        in_head = (lane >= HEAD_DIM) if s else (lane < HEAD_DIM)
        qh = jnp.where(in_head, qp, jnp.zeros_like(qp))
        sel = jnp.where((lane == 2 * s) | (lane == 2 * s + 1), 1.0, 0.0).astype(BF16)
        q_aug.append(jnp.concatenate([qh, sel], axis=1))

    acc_scr[...] = jnp.zeros_like(acc_scr)
    car_scr[...] = jnp.zeros_like(car_scr)

    def rows_of(kb):
        return pl.ds(pl.multiple_of(kb * blk, blk), blk)

    def stage1(kb, slot, masked):
        for h in range(heads):
            p = h // 2
            k_aug = jnp.concatenate([k_ref[0, rows_of(kb), p * LANES:(p + 1) * LANES], bcol_ref[p]], axis=1)
            z2 = lax.dot_general(q_aug[h], k_aug, dn_nt, preferred_element_type=F32)
            z2 = jnp.minimum(z2, Z2_MAX)
            sp = jnp.log(1.0 + jnp.exp2(z2)) * LOG2E
            if masked:
                sp = jnp.where(causal, sp, 0.0)
            z_scr[slot][h] = z2
            sp_scr[slot][h] = sp.astype(BF16)

    def stage2(slot, masked, keep=None):
        for h in range(heads):
            cs = jnp.dot(sp_scr[slot][h], u, preferred_element_type=F32)
            cs = cs + jnp.concatenate([car_scr[h]] * (blk // LANES), axis=1)
            w = jnp.exp2(cs + z_scr[slot][h])
            if masked:
                w = jnp.where(causal, w, 0.0)
            if keep is not None:
                w = jnp.where(keep, w, 0.0)
            car_scr[h] = jnp.broadcast_to(cs[:, 0:1], (blk, LANES))
            w_scr[slot][h] = w.astype(BF16)

    def stage3(slot, kb):
        for h in range(heads):
            p = h // 2
            acc_scr[h] += jnp.dot(w_scr[slot][h], v_ref[0, rows_of(kb), p * LANES:(p + 1) * LANES],
                                  preferred_element_type=F32)

    def step(t, slot):
        kb = qi - t
        stage3(slot, kb + 2)
        stage2(1 - slot, False)
        stage1(kb, slot, False)

    stage1(qi, 0, True)
    stage2(0, True)
    stage1(jnp.maximum(qi - 1, 0), 1, False)

    def two_steps(n, c):
        t = 2 + 2 * n
        step(t, 0)
        step(t + 1, 1)
        return c

    n_pairs_of_steps = jnp.maximum(qi - 1, 0) // 2
    lax.fori_loop(0, n_pairs_of_steps, two_steps, 0)
    odd = jnp.logical_and(qi >= 2, qi % 2 == 0)

    @pl.when(odd)
    def _():
        step(qi, 0)
        stage3(1, 1)
        stage2(0, False)
        stage3(0, 0)

    @pl.when(jnp.logical_not(odd))
    def _():
        stage3(0, jnp.minimum(qi, 1))
        stage2(1, False, keep=qi >= 1)
        stage3(1, 0)

    for p in range(pairs):
        o_ref[0, :, p * LANES:(p + 1) * LANES] = jnp.where(lane < HEAD_DIM, acc_scr[2 * p], acc_scr[2 * p + 1])


def _attn_call(qa, ka, va, bcol, u, *, blk, pairs):
    bsz, seq, d_att = qa.shape
    width = pairs * LANES
    n_groups = d_att // width
    kern = functools.partial(_attn_kernel, blk=blk, pairs=pairs)
    return pl.pallas_call(
        kern,
        grid=(bsz, n_groups, seq // blk),
        in_specs=[
            pl.BlockSpec((1, blk, width), lambda b, g, i: (b, i, g)),
            pl.BlockSpec((1, seq, width), lambda b, g, i: (b, 0, g)),
            pl.BlockSpec((1, seq, width), lambda b, g, i: (b, 0, g)),
            pl.BlockSpec((pairs, blk, LANES), lambda b, g, i: (g, 0, 0)),
            pl.BlockSpec((blk, blk), lambda b, g, i: (0, 0)),
        ],
        out_specs=pl.BlockSpec((1, blk, width), lambda b, g, i: (b, i, g)),
        out_shape=jax.ShapeDtypeStruct((bsz, seq, d_att), F32),
        scratch_shapes=[
            pltpu.VMEM((2 * pairs, blk, LANES), F32),
            pltpu.VMEM((2 * pairs, blk, LANES), F32),
            pltpu.VMEM((2 * pairs, blk, blk), F32),
            pltpu.VMEM((2 * pairs, blk, blk), F32),
            pltpu.VMEM((2 * pairs, blk, blk), BF16),
            pltpu.VMEM((2 * pairs, blk, blk), BF16),
            pltpu.VMEM((2 * pairs, blk, blk), BF16),
            pltpu.VMEM((2 * pairs, blk, blk), BF16),
        ],
        compiler_params=_params(3),
        name="prompt_attention",
    )(qa, ka, va, bcol, u)


def _bias_columns(sb_bias_l, blk):
    nb = sb_bias_l.astype(F32) * LOG2E
    hi = nb.astype(BF16)
    lo = (nb - hi.astype(F32)).astype(BF16)
    cols = jnp.stack([hi[0::2], lo[0::2], hi[1::2], lo[1::2]], axis=1)
    cols = jnp.pad(cols, ((0, 0), (0, LANES - 4)))
    return jnp.broadcast_to(cols[:, None, :], (cols.shape[0], blk, LANES))


def _same_head(r, c, t_new):
    assert t_new & (t_new - 1) == 0 and HEAD_DIM & (HEAD_DIM - 1) == 0
    return jnp.right_shift(r, t_new.bit_length() - 1) == jnp.right_shift(c, HEAD_DIM.bit_length() - 1)


def _sattn_kernel(pt_ref, q_ref, kn_ref, vn_ref, bias_ref, us_ref, *rest, n_pages_step, t_new, d_att):
    k_refs = rest[:n_pages_step]
    v_refs = rest[n_pages_step:2 * n_pages_step]
    o_ref = rest[2 * n_pages_step]
    qbd_scr, acc_scr, car_scr, z_scr, sp_scr, w_scr = rest[2 * n_pages_step + 1:]
    j = pl.program_id(1)
    nj = pl.num_programs(1)
    rows = N_HEADS * t_new
    bias = bias_ref[...]
    dn_nt = (((1,), (1,)), ((), ()))

    def softplus2(z2):
        return jnp.log(1.0 + jnp.exp2(z2)) * LOG2E

    @pl.when(j == 0)
    def _():
        q = q_ref[0]
        qt = jnp.concatenate([q] * N_HEADS, axis=0)
        r = lax.broadcasted_iota(jnp.int32, (rows, d_att), 0)
        c = lax.broadcasted_iota(jnp.int32, (rows, d_att), 1)
        qbd = jnp.where(_same_head(r, c, t_new), qt, 0.0).astype(BF16)
        qbd_scr[...] = qbd
        pad = jnp.zeros((PAGE_SIZE - t_new, d_att), F32)
        kn = jnp.concatenate([kn_ref[0], pad], axis=0).astype(BF16)
        vn = jnp.concatenate([vn_ref[0], pad], axis=0).astype(BF16)
        r1 = lax.broadcasted_iota(jnp.int32, (rows, PAGE_SIZE), 0)
        c1 = lax.broadcasted_iota(jnp.int32, (rows, PAGE_SIZE), 1)
        visible = c1 < jnp.bitwise_and(r1, t_new - 1)
        z2 = jnp.minimum(lax.dot_general(qbd, kn, dn_nt, preferred_element_type=F32) + bias, Z2_MAX)
        sp = jnp.where(visible, softplus2(z2), 0.0)
        cs = jnp.dot(sp.astype(BF16), us_ref[...], preferred_element_type=F32)
        w = jnp.where(visible, jnp.exp2(cs[:, :PAGE_SIZE] + z2), 0.0)
        acc_scr[...] = jnp.dot(w.astype(BF16), vn, preferred_element_type=F32)
        car_scr[...] = cs[:, PAGE_SIZE:]

    def scores(p):
        kt = k_refs[p][...].reshape(d_att, PAGE_SIZE).astype(BF16)
        z2 = jnp.minimum(jnp.dot(qbd, kt, preferred_element_type=F32) + bias, Z2_MAX)
        z_scr[p] = z2
        sp_scr[p] = softplus2(z2).astype(BF16)

    def weights(p, run):
        cs = jnp.dot(sp_scr[p], us, preferred_element_type=F32)
        w_scr[p] = jnp.exp2(cs[:, :PAGE_SIZE] + run + z_scr[p]).astype(BF16)
        return run + cs[:, PAGE_SIZE:]

    def values(p):
        vt = v_refs[p][...].reshape(d_att, PAGE_SIZE).astype(BF16)
        return lax.dot_general(w_scr[p], vt, dn_nt, preferred_element_type=F32)

    qbd = qbd_scr[...]
    us = us_ref[...]
    run = car_scr[...]
    acc = acc_scr[...]
    for s in range(n_pages_step + 2):
        p = n_pages_step - 1 - s
        if p + 2 < n_pages_step:
            acc = acc + values(p + 2)
        if 0 <= p + 1 < n_pages_step:
            run = weights(p + 1, run)
        if p >= 0:
            scores(p)
    car_scr[...] = run
    acc_scr[...] = acc

    @pl.when(j == nj - 1)
    def _():
        r = lax.broadcasted_iota(jnp.int32, (rows, d_att), 0)
        c = lax.broadcasted_iota(jnp.int32, (rows, d_att), 1)
        sel = jnp.where(_same_head(r, c, t_new), acc, 0.0)
        out = sel[0:t_new, :]
        for h in range(1, N_HEADS):
            out = out + sel[h * t_new:(h + 1) * t_new, :]
        o_ref[0] = out


def _suffix_and_total(n):
    return jnp.concatenate([-_suffix_ones(n), -jnp.ones((n, n), BF16)], axis=1)


def _sattn_call(page_table, q, k_new, v_new, bias_rows, cache_kt, cache_vt, *, layer):
    bsz, t_new, d_att = q.shape
    n_pages = page_table.shape[1]
    pps = PAGES_PER_STEP
    n_steps = n_pages // pps
    rows = N_HEADS * t_new

    def page_map(p):
        return lambda b, j, pt: (layer, pt[b, (n_steps - 1 - j) * pps + p], 0, 0, 0)

    per_b = lambda b, j, pt: (b, 0, 0)
    const2 = lambda b, j, pt: (0, 0)
    page_specs = [pl.BlockSpec((None, None, N_HEADS, HEAD_DIM, PAGE_SIZE), page_map(p)) for p in range(pps)]
    kern = functools.partial(_sattn_kernel, n_pages_step=pps, t_new=t_new, d_att=d_att)
    return pl.pallas_call(
        kern,
        grid_spec=pltpu.PrefetchScalarGridSpec(
            num_scalar_prefetch=1,
            grid=(bsz, n_steps),
            in_specs=[
                pl.BlockSpec((1, t_new, d_att), per_b),
                pl.BlockSpec((1, t_new, d_att), per_b),
                pl.BlockSpec((1, t_new, d_att), per_b),
                pl.BlockSpec((rows, PAGE_SIZE), const2),
                pl.BlockSpec((PAGE_SIZE, 2 * PAGE_SIZE), const2),
            ] + page_specs + page_specs,
            out_specs=pl.BlockSpec((1, t_new, d_att), per_b),
            scratch_shapes=[
                pltpu.VMEM((rows, d_att), BF16),
                pltpu.VMEM((rows, d_att), F32),
                pltpu.VMEM((rows, PAGE_SIZE), F32),
                pltpu.VMEM((pps, rows, PAGE_SIZE), F32),
                pltpu.VMEM((pps, rows, PAGE_SIZE), BF16),
                pltpu.VMEM((pps, rows, PAGE_SIZE), BF16),
            ],
        ),
        out_shape=jax.ShapeDtypeStruct((bsz, t_new, d_att), F32),
        compiler_params=_params(2),
        name="sample_attention",
    )(page_table, q, k_new, v_new, bias_rows, _suffix_and_total(PAGE_SIZE),
      *([cache_kt] * pps), *([cache_vt] * pps))


def _post_kernel(x_ref, yrec_ref, o_ref, sgb_ref, gate_ref, wout_ref, lng_ref, lnb_ref, out_ref,
                 *, d_rec, alpha):
    yatt = (o_ref[0] * sgb_ref[0]).astype(BF16)
    y = jnp.dot(yrec_ref[0], wout_ref[0:d_rec, :], preferred_element_type=F32)
    y = y + jnp.dot(yatt, wout_ref[d_rec:, :], preferred_element_type=F32)
    y = y * gate_ref[0]
    s = alpha * x_ref[0] + y
    out_ref[0] = _layer_norm(s) * lng_ref[...] + lnb_ref[...]


def _post_call(x, yrec, o, sgb, gate, wout, lng, lnb, *, rows, alpha):
    bsz, seq, d = x.shape
    d_rec = yrec.shape[2]
    d_att = o.shape[2]
    kern = functools.partial(_post_kernel, d_rec=d_rec, alpha=alpha)
    tile = lambda b, t: (b, t, 0)
    const2 = lambda b, t: (0, 0)
    return pl.pallas_call(
        kern,
        grid=(bsz, seq // rows),
        in_specs=[
            pl.BlockSpec((1, rows, d), tile),
            pl.BlockSpec((1, rows, d_rec), tile),
            pl.BlockSpec((1, rows, d_att), tile),
            pl.BlockSpec((1, rows, d_att), tile),
            pl.BlockSpec((1, 1, d), lambda b, t: (b, 0, 0)),
            pl.BlockSpec((d_rec + d_att, d), const2),
            pl.BlockSpec((1, d), const2),
            pl.BlockSpec((1, d), const2),
        ],
        out_specs=pl.BlockSpec((1, rows, d), tile),
        out_shape=jax.ShapeDtypeStruct((bsz, seq, d), F32),
        compiler_params=_params(2),
        name="post_mixer",
    )(x, yrec, o, sgb, gate, wout, lng, lnb)


def _suffix_ones(n):
    s = lax.broadcasted_iota(jnp.int32, (n, n), 0)
    j = lax.broadcasted_iota(jnp.int32, (n, n), 1)
    return (s >= j).astype(BF16)


def _block_diag(w):
    n, d, e = w.shape
    eye = jnp.eye(n, dtype=w.dtype)
    return (eye[:, None, :, None] * w[:, :, None, :]).reshape(n * d, n * e)


def kernel(x_prompt, x_sample, cache_k, cache_v, state_h, state_conv, page_table, c_prompt, c_sample,
           w_ada, b_ada, w_in, conv_w, conv_b, w_r, b_r, w_i, b_i, lam, sb_bias, w_out, ln_g, ln_b):
    depth = w_ada.shape[0]
    bp, seq, d = x_prompt.shape
    bs, t_new, _ = x_sample.shape
    d_rec = conv_w.shape[2]
    d_att = N_HEADS * HEAD_DIM
    alpha = (2 * depth) ** 0.25

    n_c = bp + bs
    pad_rows = (-n_c) % SUBLANES
    c_all = jnp.concatenate([c_prompt, c_sample, jnp.zeros((pad_rows, d), F32)], axis=0)
    mod = _modulation(c_all, w_ada, b_ada)

    u_prompt = -_suffix_ones(ATT_BLOCK)
    cache_kt = jnp.transpose(cache_k, (0, 1, 3, 4, 2))
    cache_vt = jnp.transpose(cache_v, (0, 1, 3, 4, 2))
    zeros_hist = jnp.zeros((bp, SUBLANES, d_rec), F32)
    zeros_h = jnp.zeros((bp, 1, d_rec), F32)
    hist_pad = jnp.zeros((depth, bs, SUBLANES - (CONV_W - 1), d_rec), F32)
    state_conv8 = jnp.concatenate([hist_pad, state_conv], axis=2)

    xp, xs = x_prompt, x_sample
    outs = [[] for _ in range(8)]
    for l in range(depth):
        shift, scale, gate = mod[l, :, :d], mod[l, :, d:2 * d], mod[l, :, 2 * d:]
        win = w_in[l].astype(BF16)
        wg = jnp.concatenate([_block_diag(w_r[l]), _block_diag(w_i[l])], axis=1).astype(BF16)
        bg = jnp.concatenate([b_r[l], b_i[l]])[None, :]
        wout = w_out[l].astype(BF16)
        weights = (win, conv_w[l], conv_b[l][None, :], wg, bg, lam[l][None, :])
        lng, lnb = ln_g[l][None, :], ln_b[l][None, :]

        sl = slice(0, bp)
        k, v, qa, ka, va, yrec, sgb, hl, cv = _pre_call(
            xp, shift[sl, None, :], scale[sl, None, :], *weights, zeros_hist, zeros_h,
            rows=PRE_ROWS, reset_first=True, q_dtype=BF16, q_scale=LOG2E / math.sqrt(HEAD_DIM))
        o = _attn_call(qa, ka, va, _bias_columns(sb_bias[l], ATT_BLOCK), u_prompt, blk=ATT_BLOCK, pairs=ATT_PAIRS)
        xp = _post_call(xp, yrec, o, sgb, gate[sl, None, :], wout, lng, lnb, rows=POST_ROWS, alpha=alpha)
        outs[0].append(k.reshape(bp, seq, N_HEADS, HEAD_DIM))
        outs[1].append(v.reshape(bp, seq, N_HEADS, HEAD_DIM))
        outs[2].append(hl[:, 0, :])
        outs[3].append(cv[:, SUBLANES - (CONV_W - 1):, :])

        sl = slice(bp, bp + bs)
        k, v, qa, _, _, yrec, sgb, hl, cv = _pre_call(
            xs, shift[sl, None, :], scale[sl, None, :], *weights, state_conv8[l], state_h[l][:, None, :],
            rows=t_new, reset_first=False, q_dtype=F32, q_scale=LOG2E / math.sqrt(HEAD_DIM))
        bias_rows = jnp.broadcast_to(jnp.repeat(sb_bias[l] * LOG2E, t_new)[:, None], (N_HEADS * t_new, LANES))
        o = _sattn_call(page_table, qa, k, v, bias_rows, cache_kt, cache_vt, layer=l)
        xs = _post_call(xs, yrec, o, sgb, gate[sl, None, :], wout, lng, lnb, rows=t_new, alpha=alpha)
        outs[4].append(k.reshape(bs, t_new, N_HEADS, HEAD_DIM))
        outs[5].append(v.reshape(bs, t_new, N_HEADS, HEAD_DIM))
        outs[6].append(hl[:, 0, :])
        outs[7].append(cv[:, SUBLANES - (CONV_W - 1):, :])

    stacked = [jnp.stack(o_) for o_ in outs]
    return (xp, xs, *stacked)
```

```python
import functools
import math

import jax
import jax.numpy as jnp
from jax import lax
from jax.experimental import pallas as pl
from jax.experimental.pallas import tpu as pltpu

F32 = jnp.float32
BF16 = jnp.bfloat16

N_HEADS = 8
HEAD_DIM = 64
N_REC_BLOCKS = 8
CONV_W = 4
RG_C = 8.0
PAGE_SIZE = 128
LN_EPS = 1e-5
LOG2E = math.log2(math.e)
Z2_MAX = 120.0

SUBLANES = 8
LANES = 128
VMEM_LIMIT = 56 * 1024 * 1024

PRE_ROWS = 512
POST_ROWS = 512
ATT_BLOCK = 256
ATT_PAIRS = 2
ATT_UNROLL = 8
PAGES_PER_STEP = 16


def _sigmoid(x):
    return 1.0 / (1.0 + jnp.exp(-x))


def _layer_norm(x):
    mu = jnp.mean(x, axis=-1, keepdims=True)
    xc = x - mu
    var = jnp.mean(xc * xc, axis=-1, keepdims=True)
    return xc * lax.rsqrt(var + LN_EPS)


def _params(n_axes):
    return pltpu.CompilerParams(dimension_semantics=("arbitrary",) * n_axes,
                                vmem_limit_bytes=VMEM_LIMIT)


def _mod_kernel(c_ref, w_ref, b_ref, o_ref):
    c = c_ref[...]
    sc = c * _sigmoid(c)
    o_ref[0] = jnp.dot(sc, w_ref[0], precision=lax.Precision.HIGHEST,
                       preferred_element_type=F32) + b_ref[0]


def _modulation(c_all, w_ada, b_ada):
    depth, d, d3 = w_ada.shape
    rows = c_all.shape[0]
    n_col = d3 // d
    return pl.pallas_call(
        _mod_kernel,
        grid=(depth, n_col),
        in_specs=[
            pl.BlockSpec((rows, d), lambda l, n: (0, 0)),
            pl.BlockSpec((1, d, d), lambda l, n: (l, 0, n)),
            pl.BlockSpec((1, 1, d), lambda l, n: (l, 0, n)),
        ],
        out_specs=pl.BlockSpec((1, rows, d), lambda l, n: (l, 0, n)),
        out_shape=jax.ShapeDtypeStruct((depth, rows, d3), F32),
        compiler_params=_params(2),
        name="adaln_modulation",
    )(c_all, w_ada, b_ada.reshape(depth, 1, d3))


def _pre_kernel(x_ref, shift_ref, scale_ref, win_ref, cw_ref, cb_ref, wg_ref, bg_ref, lam_ref,
                hist0_ref, h0_ref,
                k_ref, v_ref, qa_ref, ka_ref, va_ref, yrec_ref, sgb_ref, hlast_ref, conv_ref,
                xp_scr, h_scr, a_scr, b_scr, sga_scr, *, rows, d_rec, d_att, reset_first, q_scale):
    ti = pl.program_id(1)

    @pl.when(ti == 0)
    def _():
        xp_scr[0:SUBLANES, :] = hist0_ref[0]
        h_scr[...] = jnp.broadcast_to(h0_ref[0], (SUBLANES, d_rec))

    x = x_ref[0]
    u = _layer_norm(x) * (1.0 + scale_ref[0]) + shift_ref[0]
    z = jnp.dot(u.astype(BF16), win_ref[...], preferred_element_type=F32)
    o1, o2 = d_rec, 2 * d_rec
    o3, o4, o5 = o2 + d_att, o2 + 2 * d_att, o2 + 3 * d_att
    xa = z[:, :o1]
    ga = z[:, o1:o2]
    q = z[:, o2:o3]
    k = z[:, o3:o4]
    v = z[:, o4:o5]
    gb = z[:, o5:]
    k_ref[0] = k
    v_ref[0] = v
    qa_ref[0] = (q * q_scale).astype(qa_ref.dtype)
    ka_ref[0] = k.astype(BF16)
    va_ref[0] = v.astype(BF16)
    sgb_ref[0] = gb * _sigmoid(gb)
    sga_scr[...] = ga * _sigmoid(ga)

    xp_scr[SUBLANES:SUBLANES + rows, :] = xa
    cw = cw_ref[...]
    xc = cb_ref[...] + cw[3:4, :] * xa
    for j in range(CONV_W - 1):
        off = SUBLANES - (CONV_W - 1) + j
        xc = xc + cw[j:j + 1, :] * xp_scr[off:off + rows, :]
    last_rows = xa[rows - SUBLANES:, :]
    xp_scr[0:SUBLANES, :] = last_rows
    conv_ref[0] = last_rows

    gates = jnp.dot(xc.astype(BF16), wg_ref[...], preferred_element_type=F32) + bg_ref[...]
    r = _sigmoid(gates[:, :d_rec])
    i = _sigmoid(gates[:, d_rec:])
    lam = lam_ref[...]
    c_lam = -RG_C * (jnp.maximum(-lam, 0.0) + jnp.log1p(jnp.exp(-jnp.abs(lam))))
    a = jnp.exp(c_lam * r)
    mult = jnp.sqrt(1.0 - a * a)
    if reset_first:
        row = lax.broadcasted_iota(jnp.int32, (rows, d_rec), 0)
        mult = jnp.where(jnp.logical_and(row == 0, ti == 0), 1.0, mult)
    a_scr[...] = a
    b_scr[...] = mult * (i * xc)

    srow = lax.broadcasted_iota(jnp.int32, (SUBLANES, d_rec), 0)

    def group(g, hprev):
        sl = pl.ds(pl.multiple_of(g * SUBLANES, SUBLANES), SUBLANES)
        ag = a_scr[sl, :]
        bg = b_scr[sl, :]
        for s in (1, 2, 4):
            keep = srow >= s
            a_sh = jnp.where(keep, pltpu.roll(ag, s, 0), 1.0)
            b_sh = jnp.where(keep, pltpu.roll(bg, s, 0), 0.0)
            bg = ag * b_sh + bg
            ag = ag * a_sh
        h = ag * hprev + bg
        b_scr[sl, :] = h
        return jnp.broadcast_to(h[SUBLANES - 1:SUBLANES, :], (SUBLANES, d_rec))

    hfin = lax.fori_loop(0, rows // SUBLANES, group, h_scr[...])
    h_scr[...] = hfin
    hlast_ref[0] = hfin[0:1, :]
    yrec_ref[0] = (b_scr[...] * sga_scr[...]).astype(BF16)


def _pre_call(x, shift, scale, win, cw, cb, wg, bg, lam, hist0, h0, *, rows, reset_first, q_dtype, q_scale):
    bsz, seq, d = x.shape
    d_in = win.shape[1]
    d_rec = cw.shape[1]
    d_att = (d_in - 2 * d_rec) // 4
    nt = seq // rows
    kern = functools.partial(_pre_kernel, rows=rows, d_rec=d_rec, d_att=d_att, reset_first=reset_first,
                             q_scale=q_scale)
    const2 = lambda b, t: (0, 0)
    per_b = lambda b, t: (b, 0, 0)
    tile = lambda b, t: (b, t, 0)
    out_shapes = (
        jax.ShapeDtypeStruct((bsz, seq, d_att), F32),
        jax.ShapeDtypeStruct((bsz, seq, d_att), F32),
        jax.ShapeDtypeStruct((bsz, seq, d_att), q_dtype),
        jax.ShapeDtypeStruct((bsz, seq, d_att), BF16),
        jax.ShapeDtypeStruct((bsz, seq, d_att), BF16),
        jax.ShapeDtypeStruct((bsz, seq, d_rec), BF16),
        jax.ShapeDtypeStruct((bsz, seq, d_att), F32),
        jax.ShapeDtypeStruct((bsz, 1, d_rec), F32),
        jax.ShapeDtypeStruct((bsz, SUBLANES, d_rec), F32),
    )
    out_specs = (
        pl.BlockSpec((1, rows, d_att), tile),
        pl.BlockSpec((1, rows, d_att), tile),
        pl.BlockSpec((1, rows, d_att), tile),
        pl.BlockSpec((1, rows, d_att), tile),
        pl.BlockSpec((1, rows, d_att), tile),
        pl.BlockSpec((1, rows, d_rec), tile),
        pl.BlockSpec((1, rows, d_att), tile),
        pl.BlockSpec((1, 1, d_rec), per_b),
        pl.BlockSpec((1, SUBLANES, d_rec), per_b),
    )
    in_specs = [
        pl.BlockSpec((1, rows, d), tile),
        pl.BlockSpec((1, 1, d), per_b),
        pl.BlockSpec((1, 1, d), per_b),
        pl.BlockSpec((d, d_in), const2),
        pl.BlockSpec((CONV_W, d_rec), const2),
        pl.BlockSpec((1, d_rec), const2),
        pl.BlockSpec((d_rec, 2 * d_rec), const2),
        pl.BlockSpec((1, 2 * d_rec), const2),
        pl.BlockSpec((1, d_rec), const2),
        pl.BlockSpec((1, SUBLANES, d_rec), per_b),
        pl.BlockSpec((1, 1, d_rec), per_b),
    ]
    return pl.pallas_call(
        kern,
        grid=(bsz, nt),
        in_specs=in_specs,
        out_specs=out_specs,
        out_shape=out_shapes,
        scratch_shapes=[
            pltpu.VMEM((SUBLANES + rows, d_rec), F32),
            pltpu.VMEM((SUBLANES, d_rec), F32),
            pltpu.VMEM((rows, d_rec), F32),
            pltpu.VMEM((rows, d_rec), F32),
            pltpu.VMEM((rows, d_rec), F32),
        ],
        compiler_params=_params(2),
        name="pre_mixer",
    )(x, shift, scale, win, cw, cb, wg, bg, lam, hist0, h0)


def _attn_kernel(q_ref, k_ref, v_ref, bcol_ref, u_ref, o_ref, acc_scr, car_scr,
                 z0_scr, z1_scr, sp0_scr, sp1_scr, w0_scr, w1_scr, *, blk, pairs):
    qi = pl.program_id(2)
    heads = 2 * pairs
    z_scr, sp_scr, w_scr = (z0_scr, z1_scr), (sp0_scr, sp1_scr), (w0_scr, w1_scr)
    lane = lax.broadcasted_iota(jnp.int32, (blk, LANES), 1)
    row = lax.broadcasted_iota(jnp.int32, (blk, blk), 0)
    col = lax.broadcasted_iota(jnp.int32, (blk, blk), 1)
    causal = col < row
    u = u_ref[...]
    dn_nt = (((1,), (1,)), ((), ()))

    q_aug = []
    for h in range(heads):
        p, s = divmod(h, 2)
        qp = q_ref[0, :, p * LANES:(p + 1) * LANES]
        in_head = (lane >= HEAD_DIM) if s else (lane < HEAD_DIM)
        qh = jnp.where(in_head, qp, jnp.zeros_like(qp))
        sel = jnp.where((lane == 2 * s) | (lane == 2 * s + 1), 1.0, 0.0).astype(BF16)
        q_aug.append(jnp.concatenate([qh, sel], axis=1))

    acc_scr[...] = jnp.zeros_like(acc_scr)
    car_scr[...] = jnp.zeros_like(car_scr)

    def rows_of(kb):
        return pl.ds(pl.multiple_of(kb * blk, blk), blk)

    def stage1(kb, slot, masked):
        for h in range(heads):
            p = h // 2
            k_aug = jnp.concatenate([k_ref[0, rows_of(kb), p * LANES:(p + 1) * LANES], bcol_ref[p]], axis=1)
            z2 = lax.dot_general(q_aug[h], k_aug, dn_nt, preferred_element_type=F32)
            z2 = jnp.minimum(z2, Z2_MAX)
            sp = jnp.log(1.0 + jnp.exp2(z2)) * LOG2E
            if masked:
                sp = jnp.where(causal, sp, 0.0)
            z_scr[slot][h] = z2
            sp_scr[slot][h] = sp.astype(BF16)

    def stage2(slot, masked, keep=None):
        for h in range(heads):
            cs = jnp.dot(sp_scr[slot][h], u, preferred_element_type=F32)
            cs = cs + jnp.concatenate([car_scr[h]] * (blk // LANES), axis=1)
            w = jnp.exp2(cs + z_scr[slot][h])
            if masked:
                w = jnp.where(causal, w, 0.0)
            if keep is not None:
                w = jnp.where(keep, w, 0.0)
            car_scr[h] = jnp.broadcast_to(cs[:, 0:1], (blk, LANES))
            w_scr[slot][h] = w.astype(BF16)

    def stage3(slot, kb):
        for h in range(heads):
            p = h // 2
            acc_scr[h] += jnp.dot(w_scr[slot][h], v_ref[0, rows_of(kb), p * LANES:(p + 1) * LANES],
                                  preferred_element_type=F32)

    def step(t, slot):
        kb = qi - t
        stage3(slot, kb + 2)
        stage2(1 - slot, False)
        stage1(kb, slot, False)

    stage1(qi, 0, True)
    stage2(0, True)
    stage1(jnp.maximum(qi - 1, 0), 1, False)

    def steps_from(t0, count):
        for i in range(count):
            step(t0 + i, i % 2)

    n_steps = jnp.maximum(qi - 1, 0)
    n_long = n_steps // ATT_UNROLL

    def long_body(n, c):
        steps_from(2 + ATT_UNROLL * n, ATT_UNROLL)
        return c

    def pair_body(n, c):
        steps_from(2 + ATT_UNROLL * n_long + 2 * n, 2)
        return c

    lax.fori_loop(0, n_long, long_body, 0)
    lax.fori_loop(0, (n_steps - ATT_UNROLL * n_long) // 2, pair_body, 0)
    odd = jnp.logical_and(qi >= 2, qi % 2 == 0)

    @pl.when(odd)
    def _():
        step(qi, 0)
        stage3(1, 1)
        stage2(0, False)
        stage3(0, 0)

    @pl.when(jnp.logical_not(odd))
    def _():
        stage3(0, jnp.minimum(qi, 1))
        stage2(1, False, keep=qi >= 1)
        stage3(1, 0)

    for p in range(pairs):
        o_ref[0, :, p * LANES:(p + 1) * LANES] = jnp.where(lane < HEAD_DIM, acc_scr[2 * p], acc_scr[2 * p + 1])


def _attn_call(qa, ka, va, bcol, u, *, blk, pairs):
    bsz, seq, d_att = qa.shape
    width = pairs * LANES
    n_groups = d_att // width
    kern = functools.partial(_attn_kernel, blk=blk, pairs=pairs)
    return pl.pallas_call(
        kern,
        grid=(bsz, n_groups, seq // blk),
        in_specs=[
            pl.BlockSpec((1, blk, width), lambda b, g, i: (b, i, g)),
            pl.BlockSpec((1, seq, width), lambda b, g, i: (b, 0, g)),
            pl.BlockSpec((1, seq, width), lambda b, g, i: (b, 0, g)),
            pl.BlockSpec((pairs, blk, LANES), lambda b, g, i: (g, 0, 0)),
            pl.BlockSpec((blk, blk), lambda b, g, i: (0, 0)),
        ],
        out_specs=pl.BlockSpec((1, blk, width), lambda b, g, i: (b, i, g)),
        out_shape=jax.ShapeDtypeStruct((bsz, seq, d_att), F32),
        scratch_shapes=[
            pltpu.VMEM((2 * pairs, blk, LANES), F32),
            pltpu.VMEM((2 * pairs, blk, LANES), F32),
            pltpu.VMEM((2 * pairs, blk, blk), F32),
            pltpu.VMEM((2 * pairs, blk, blk), F32),
            pltpu.VMEM((2 * pairs, blk, blk), BF16),
            pltpu.VMEM((2 * pairs, blk, blk), BF16),
            pltpu.VMEM((2 * pairs, blk, blk), BF16),
            pltpu.VMEM((2 * pairs, blk, blk), BF16),
        ],
        compiler_params=_params(3),
        name="prompt_attention",
    )(qa, ka, va, bcol, u)


def _bias_columns(sb_bias_l, blk):
    nb = sb_bias_l.astype(F32) * LOG2E
    hi = nb.astype(BF16)
    lo = (nb - hi.astype(F32)).astype(BF16)
    cols = jnp.stack([hi[0::2], lo[0::2], hi[1::2], lo[1::2]], axis=1)
    cols = jnp.pad(cols, ((0, 0), (0, LANES - 4)))
    return jnp.broadcast_to(cols[:, None, :], (cols.shape[0], blk, LANES))


def _same_head(r, c, t_new):
    assert t_new & (t_new - 1) == 0 and HEAD_DIM & (HEAD_DIM - 1) == 0
    return jnp.right_shift(r, t_new.bit_length() - 1) == jnp.right_shift(c, HEAD_DIM.bit_length() - 1)


def _sattn_kernel(pt_ref, q_ref, kn_ref, vn_ref, bias_ref, us_ref, *rest, n_pages_step, t_new, d_att):
    k_refs = rest[:n_pages_step]
    v_refs = rest[n_pages_step:2 * n_pages_step]
    o_ref = rest[2 * n_pages_step]
    qbd_scr, acc_scr, car_scr, z_scr, sp_scr, w_scr = rest[2 * n_pages_step + 1:]
    j = pl.program_id(1)
    nj = pl.num_programs(1)
    rows = N_HEADS * t_new
    bias = bias_ref[...]
    dn_nt = (((1,), (1,)), ((), ()))

    def softplus2(z2):
        return jnp.log(1.0 + jnp.exp2(z2)) * LOG2E

    @pl.when(j == 0)
    def _():
        q = q_ref[0]
        qt = jnp.concatenate([q] * N_HEADS, axis=0)
        r = lax.broadcasted_iota(jnp.int32, (rows, d_att), 0)
        c = lax.broadcasted_iota(jnp.int32, (rows, d_att), 1)
        qbd = jnp.where(_same_head(r, c, t_new), qt, 0.0).astype(BF16)
        qbd_scr[...] = qbd
        pad = jnp.zeros((PAGE_SIZE - t_new, d_att), F32)
        kn = jnp.concatenate([kn_ref[0], pad], axis=0).astype(BF16)
        vn = jnp.concatenate([vn_ref[0], pad], axis=0).astype(BF16)
        r1 = lax.broadcasted_iota(jnp.int32, (rows, PAGE_SIZE), 0)
        c1 = lax.broadcasted_iota(jnp.int32, (rows, PAGE_SIZE), 1)
        visible = c1 < jnp.bitwise_and(r1, t_new - 1)
        z2 = jnp.minimum(lax.dot_general(qbd, kn, dn_nt, preferred_element_type=F32) + bias, Z2_MAX)
        sp = jnp.where(visible, softplus2(z2), 0.0)
        cs = jnp.dot(sp.astype(BF16), us_ref[...], preferred_element_type=F32)
        w = jnp.where(visible, jnp.exp2(cs[:, :PAGE_SIZE] + z2), 0.0)
        acc_scr[...] = jnp.dot(w.astype(BF16), vn, preferred_element_type=F32)
        car_scr[...] = cs[:, PAGE_SIZE:]

    def scores(p):
        kt = k_refs[p][...].reshape(d_att, PAGE_SIZE).astype(BF16)
        z2 = jnp.minimum(jnp.dot(qbd, kt, preferred_element_type=F32) + bias, Z2_MAX)
        z_scr[p] = z2
        sp_scr[p] = softplus2(z2).astype(BF16)

    def weights(p, run):
        cs = jnp.dot(sp_scr[p], us, preferred_element_type=F32)
        w_scr[p] = jnp.exp2(cs[:, :PAGE_SIZE] + run + z_scr[p]).astype(BF16)
        return run + cs[:, PAGE_SIZE:]

    def values(p):
        vt = v_refs[p][...].reshape(d_att, PAGE_SIZE).astype(BF16)
        return lax.dot_general(w_scr[p], vt, dn_nt, preferred_element_type=F32)

    qbd = qbd_scr[...]
    us = us_ref[...]
    run = car_scr[...]
    acc = acc_scr[...]
    for s in range(n_pages_step + 2):
        p = n_pages_step - 1 - s
        if p + 2 < n_pages_step:
            acc = acc + values(p + 2)
        if 0 <= p + 1 < n_pages_step:
            run = weights(p + 1, run)
        if p >= 0:
            scores(p)
    car_scr[...] = run
    acc_scr[...] = acc

    @pl.when(j == nj - 1)
    def _():
        r = lax.broadcasted_iota(jnp.int32, (rows, d_att), 0)
        c = lax.broadcasted_iota(jnp.int32, (rows, d_att), 1)
        sel = jnp.where(_same_head(r, c, t_new), acc, 0.0)
        out = sel[0:t_new, :]
        for h in range(1, N_HEADS):
            out = out + sel[h * t_new:(h + 1) * t_new, :]
        o_ref[0] = out


def _suffix_and_total(n):
    return jnp.concatenate([-_suffix_ones(n), -jnp.ones((n, n), BF16)], axis=1)


def _sattn_call(page_table, q, k_new, v_new, bias_rows, cache_kt, cache_vt, *, layer):
    bsz, t_new, d_att = q.shape
    n_pages = page_table.shape[1]
    pps = PAGES_PER_STEP
    n_steps = n_pages // pps
    rows = N_HEADS * t_new

    def page_map(p):
        return lambda b, j, pt: (layer, pt[b, (n_steps - 1 - j) * pps + p], 0, 0, 0)

    per_b = lambda b, j, pt: (b, 0, 0)
    const2 = lambda b, j, pt: (0, 0)
    page_specs = [pl.BlockSpec((None, None, N_HEADS, HEAD_DIM, PAGE_SIZE), page_map(p)) for p in range(pps)]
    kern = functools.partial(_sattn_kernel, n_pages_step=pps, t_new=t_new, d_att=d_att)
    return pl.pallas_call(
        kern,
        grid_spec=pltpu.PrefetchScalarGridSpec(
            num_scalar_prefetch=1,
            grid=(bsz, n_steps),
            in_specs=[
                pl.BlockSpec((1, t_new, d_att), per_b),
                pl.BlockSpec((1, t_new, d_att), per_b),
                pl.BlockSpec((1, t_new, d_att), per_b),
                pl.BlockSpec((rows, PAGE_SIZE), const2),
                pl.BlockSpec((PAGE_SIZE, 2 * PAGE_SIZE), const2),
            ] + page_specs + page_specs,
            out_specs=pl.BlockSpec((1, t_new, d_att), per_b),
            scratch_shapes=[
                pltpu.VMEM((rows, d_att), BF16),
                pltpu.VMEM((rows, d_att), F32),
                pltpu.VMEM((rows, PAGE_SIZE), F32),
                pltpu.VMEM((pps, rows, PAGE_SIZE), F32),
                pltpu.VMEM((pps, rows, PAGE_SIZE), BF16),
                pltpu.VMEM((pps, rows, PAGE_SIZE), BF16),
            ],
        ),
        out_shape=jax.ShapeDtypeStruct((bsz, t_new, d_att), F32),
        compiler_params=_params(2),
        name="sample_attention",
    )(page_table, q, k_new, v_new, bias_rows, _suffix_and_total(PAGE_SIZE),
      *([cache_kt] * pps), *([cache_vt] * pps))


def _post_kernel(x_ref, yrec_ref, o_ref, sgb_ref, gate_ref, wout_ref, lng_ref, lnb_ref, out_ref,
                 *, d_rec, alpha):
    yatt = (o_ref[0] * sgb_ref[0]).astype(BF16)
    y = jnp.dot(yrec_ref[0], wout_ref[0:d_rec, :], preferred_element_type=F32)
    y = y + jnp.dot(yatt, wout_ref[d_rec:, :], preferred_element_type=F32)
    y = y * gate_ref[0]
    s = alpha * x_ref[0] + y
    out_ref[0] = _layer_norm(s) * lng_ref[...] + lnb_ref[...]


def _post_call(x, yrec, o, sgb, gate, wout, lng, lnb, *, rows, alpha):
    bsz, seq, d = x.shape
    d_rec = yrec.shape[2]
    d_att = o.shape[2]
    kern = functools.partial(_post_kernel, d_rec=d_rec, alpha=alpha)
    tile = lambda b, t: (b, t, 0)
    const2 = lambda b, t: (0, 0)
    return pl.pallas_call(
        kern,
        grid=(bsz, seq // rows),
        in_specs=[
            pl.BlockSpec((1, rows, d), tile),
            pl.BlockSpec((1, rows, d_rec), tile),
            pl.BlockSpec((1, rows, d_att), tile),
            pl.BlockSpec((1, rows, d_att), tile),
            pl.BlockSpec((1, 1, d), lambda b, t: (b, 0, 0)),
            pl.BlockSpec((d_rec + d_att, d), const2),
            pl.BlockSpec((1, d), const2),
            pl.BlockSpec((1, d), const2),
        ],
        out_specs=pl.BlockSpec((1, rows, d), tile),
        out_shape=jax.ShapeDtypeStruct((bsz, seq, d), F32),
        compiler_params=_params(2),
        name="post_mixer",
    )(x, yrec, o, sgb, gate, wout, lng, lnb)


def _suffix_ones(n):
    s = lax.broadcasted_iota(jnp.int32, (n, n), 0)
    j = lax.broadcasted_iota(jnp.int32, (n, n), 1)
    return (s >= j).astype(BF16)


def _block_diag(w):
    n, d, e = w.shape
    eye = jnp.eye(n, dtype=w.dtype)
    return (eye[:, None, :, None] * w[:, :, None, :]).reshape(n * d, n * e)


def kernel(x_prompt, x_sample, cache_k, cache_v, state_h, state_conv, page_table, c_prompt, c_sample,
           w_ada, b_ada, w_in, conv_w, conv_b, w_r, b_r, w_i, b_i, lam, sb_bias, w_out, ln_g, ln_b):
    depth = w_ada.shape[0]
    bp, seq, d = x_prompt.shape
    bs, t_new, _ = x_sample.shape
    d_rec = conv_w.shape[2]
    d_att = N_HEADS * HEAD_DIM
    alpha = (2 * depth) ** 0.25

    n_c = bp + bs
    pad_rows = (-n_c) % SUBLANES
    c_all = jnp.concatenate([c_prompt, c_sample, jnp.zeros((pad_rows, d), F32)], axis=0)
    mod = _modulation(c_all, w_ada, b_ada)

    u_prompt = -_suffix_ones(ATT_BLOCK)
    cache_kt = jnp.transpose(cache_k, (0, 1, 3, 4, 2))
    cache_vt = jnp.transpose(cache_v, (0, 1, 3, 4, 2))
    zeros_hist = jnp.zeros((bp, SUBLANES, d_rec), F32)
    zeros_h = jnp.zeros((bp, 1, d_rec), F32)
    hist_pad = jnp.zeros((depth, bs, SUBLANES - (CONV_W - 1), d_rec), F32)
    state_conv8 = jnp.concatenate([hist_pad, state_conv], axis=2)

    xp, xs = x_prompt, x_sample
    outs = [[] for _ in range(8)]
    for l in range(depth):
        shift, scale, gate = mod[l, :, :d], mod[l, :, d:2 * d], mod[l, :, 2 * d:]
        win = w_in[l].astype(BF16)
        wg = jnp.concatenate([_block_diag(w_r[l]), _block_diag(w_i[l])], axis=1).astype(BF16)
        bg = jnp.concatenate([b_r[l], b_i[l]])[None, :]
        wout = w_out[l].astype(BF16)
        weights = (win, conv_w[l], conv_b[l][None, :], wg, bg, lam[l][None, :])
        lng, lnb = ln_g[l][None, :], ln_b[l][None, :]

        sl = slice(0, bp)
        k, v, qa, ka, va, yrec, sgb, hl, cv = _pre_call(
            xp, shift[sl, None, :], scale[sl, None, :], *weights, zeros_hist, zeros_h,
            rows=PRE_ROWS, reset_first=True, q_dtype=BF16, q_scale=LOG2E / math.sqrt(HEAD_DIM))
        o = _attn_call(qa, ka, va, _bias_columns(sb_bias[l], ATT_BLOCK), u_prompt, blk=ATT_BLOCK, pairs=ATT_PAIRS)
        xp = _post_call(xp, yrec, o, sgb, gate[sl, None, :], wout, lng, lnb, rows=POST_ROWS, alpha=alpha)
        outs[0].append(k.reshape(bp, seq, N_HEADS, HEAD_DIM))
        outs[1].append(v.reshape(bp, seq, N_HEADS, HEAD_DIM))
        outs[2].append(hl[:, 0, :])
        outs[3].append(cv[:, SUBLANES - (CONV_W - 1):, :])

        sl = slice(bp, bp + bs)
        k, v, qa, _, _, yrec, sgb, hl, cv = _pre_call(
            xs, shift[sl, None, :], scale[sl, None, :], *weights, state_conv8[l], state_h[l][:, None, :],
            rows=t_new, reset_first=False, q_dtype=F32, q_scale=LOG2E / math.sqrt(HEAD_DIM))
        bias_rows = jnp.broadcast_to(jnp.repeat(sb_bias[l] * LOG2E, t_new)[:, None], (N_HEADS * t_new, LANES))
        o = _sattn_call(page_table, qa, k, v, bias_rows, cache_kt, cache_vt, layer=l)
        xs = _post_call(xs, yrec, o, sgb, gate[sl, None, :], wout, lng, lnb, rows=t_new, alpha=alpha)
        outs[4].append(k.reshape(bs, t_new, N_HEADS, HEAD_DIM))
        outs[5].append(v.reshape(bs, t_new, N_HEADS, HEAD_DIM))
        outs[6].append(hl[:, 0, :])
        outs[7].append(cv[:, SUBLANES - (CONV_W - 1):, :])

    stacked = [jnp.stack(o_) for o_ in outs]
    return (xp, xs, *stacked)
```

```python
import functools
import math

import jax
import jax.numpy as jnp
from jax import lax
from jax.experimental import pallas as pl
from jax.experimental.pallas import tpu as pltpu

F32 = jnp.float32
BF16 = jnp.bfloat16

N_HEADS = 8
HEAD_DIM = 64
N_REC_BLOCKS = 8
CONV_W = 4
RG_C = 8.0
PAGE_SIZE = 128
LN_EPS = 1e-5
LOG2E = math.log2(math.e)
Z2_MAX = 120.0

SUBLANES = 8
LANES = 128
VMEM_LIMIT = 56 * 1024 * 1024

PRE_ROWS = 512
POST_ROWS = 512
ATT_BLOCK = 256
ATT_PAIRS = 2
ATT_UNROLL = 8
PAGES_PER_STEP = 32


def _sigmoid(x):
    return 1.0 / (1.0 + jnp.exp(-x))


def _layer_norm(x):
    mu = jnp.mean(x, axis=-1, keepdims=True)
    xc = x - mu
    var = jnp.mean(xc * xc, axis=-1, keepdims=True)
    return xc * lax.rsqrt(var + LN_EPS)


def _params(n_axes):
    return pltpu.CompilerParams(dimension_semantics=("arbitrary",) * n_axes,
                                vmem_limit_bytes=VMEM_LIMIT)


def _mod_kernel(c_ref, w_ref, b_ref, o_ref):
    c = c_ref[...]
    sc = c * _sigmoid(c)
    o_ref[0] = jnp.dot(sc, w_ref[0], precision=lax.Precision.HIGHEST,
                       preferred_element_type=F32) + b_ref[0]


def _modulation(c_all, w_ada, b_ada):
    depth, d, d3 = w_ada.shape
    rows = c_all.shape[0]
    n_col = d3 // d
    return pl.pallas_call(
        _mod_kernel,
        grid=(depth, n_col),
        in_specs=[
            pl.BlockSpec((rows, d), lambda l, n: (0, 0)),
            pl.BlockSpec((1, d, d), lambda l, n: (l, 0, n)),
            pl.BlockSpec((1, 1, d), lambda l, n: (l, 0, n)),
        ],
        out_specs=pl.BlockSpec((1, rows, d), lambda l, n: (l, 0, n)),
        out_shape=jax.ShapeDtypeStruct((depth, rows, d3), F32),
        compiler_params=_params(2),
        name="adaln_modulation",
    )(c_all, w_ada, b_ada.reshape(depth, 1, d3))


def _pre_kernel(x_ref, shift_ref, scale_ref, win_ref, cw_ref, cb_ref, wg_ref, bg_ref, lam_ref,
                hist0_ref, h0_ref,
                k_ref, v_ref, qa_ref, ka_ref, va_ref, yrec_ref, sgb_ref, hlast_ref, conv_ref,
                xp_scr, h_scr, a_scr, b_scr, sga_scr, *, rows, d_rec, d_att, reset_first, q_scale):
    ti = pl.program_id(1)

    @pl.when(ti == 0)
    def _():
        xp_scr[0:SUBLANES, :] = hist0_ref[0]
        h_scr[...] = jnp.broadcast_to(h0_ref[0], (SUBLANES, d_rec))

    x = x_ref[0]
    u = _layer_norm(x) * (1.0 + scale_ref[0]) + shift_ref[0]
    z = jnp.dot(u.astype(BF16), win_ref[...], preferred_element_type=F32)
    o1, o2 = d_rec, 2 * d_rec
    o3, o4, o5 = o2 + d_att, o2 + 2 * d_att, o2 + 3 * d_att
    xa = z[:, :o1]
    ga = z[:, o1:o2]
    q = z[:, o2:o3]
    k = z[:, o3:o4]
    v = z[:, o4:o5]
    gb = z[:, o5:]
    k_ref[0] = k
    v_ref[0] = v
    qa_ref[0] = (q * q_scale).astype(qa_ref.dtype)
    ka_ref[0] = k.astype(BF16)
    va_ref[0] = v.astype(BF16)
    sgb_ref[0] = gb * _sigmoid(gb)
    sga_scr[...] = ga * _sigmoid(ga)

    xp_scr[SUBLANES:SUBLANES + rows, :] = xa
    cw = cw_ref[...]
    xc = cb_ref[...] + cw[3:4, :] * xa
    for j in range(CONV_W - 1):
        off = SUBLANES - (CONV_W - 1) + j
        xc = xc + cw[j:j + 1, :] * xp_scr[off:off + rows, :]
    last_rows = xa[rows - SUBLANES:, :]
    xp_scr[0:SUBLANES, :] = last_rows
    conv_ref[0] = last_rows

    gates = jnp.dot(xc.astype(BF16), wg_ref[...], preferred_element_type=F32) + bg_ref[...]
    r = _sigmoid(gates[:, :d_rec])
    i = _sigmoid(gates[:, d_rec:])
    lam = lam_ref[...]
    c_lam = -RG_C * (jnp.maximum(-lam, 0.0) + jnp.log1p(jnp.exp(-jnp.abs(lam))))
    a = jnp.exp(c_lam * r)
    mult = jnp.sqrt(1.0 - a * a)
    if reset_first:
        row = lax.broadcasted_iota(jnp.int32, (rows, d_rec), 0)
        mult = jnp.where(jnp.logical_and(row == 0, ti == 0), 1.0, mult)
    a_scr[...] = a
    b_scr[...] = mult * (i * xc)

    srow = lax.broadcasted_iota(jnp.int32, (SUBLANES, d_rec), 0)

    def group(g, hprev):
        sl = pl.ds(pl.multiple_of(g * SUBLANES, SUBLANES), SUBLANES)
        ag = a_scr[sl, :]
        bg = b_scr[sl, :]
        for s in (1, 2, 4):
            keep = srow >= s
            a_sh = jnp.where(keep, pltpu.roll(ag, s, 0), 1.0)
            b_sh = jnp.where(keep, pltpu.roll(bg, s, 0), 0.0)
            bg = ag * b_sh + bg
            ag = ag * a_sh
        h = ag * hprev + bg
        b_scr[sl, :] = h
        return jnp.broadcast_to(h[SUBLANES - 1:SUBLANES, :], (SUBLANES, d_rec))

    hfin = lax.fori_loop(0, rows // SUBLANES, group, h_scr[...])
    h_scr[...] = hfin
    hlast_ref[0] = hfin[0:1, :]
    yrec_ref[0] = (b_scr[...] * sga_scr[...]).astype(BF16)


def _pre_call(x, shift, scale, win, cw, cb, wg, bg, lam, hist0, h0, *, rows, reset_first, q_dtype, q_scale):
    bsz, seq, d = x.shape
    d_in = win.shape[1]
    d_rec = cw.shape[1]
    d_att = (d_in - 2 * d_rec) // 4
    nt = seq // rows
    kern = functools.partial(_pre_kernel, rows=rows, d_rec=d_rec, d_att=d_att, reset_first=reset_first,
                             q_scale=q_scale)
    const2 = lambda b, t: (0, 0)
    per_b = lambda b, t: (b, 0, 0)
    tile = lambda b, t: (b, t, 0)
    out_shapes = (
        jax.ShapeDtypeStruct((bsz, seq, d_att), F32),
        jax.ShapeDtypeStruct((bsz, seq, d_att), F32),
        jax.ShapeDtypeStruct((bsz, seq, d_att), q_dtype),
        jax.ShapeDtypeStruct((bsz, seq, d_att), BF16),
        jax.ShapeDtypeStruct((bsz, seq, d_att), BF16),
        jax.ShapeDtypeStruct((bsz, seq, d_rec), BF16),
        jax.ShapeDtypeStruct((bsz, seq, d_att), F32),
        jax.ShapeDtypeStruct((bsz, 1, d_rec), F32),
        jax.ShapeDtypeStruct((bsz, SUBLANES, d_rec), F32),
    )
    out_specs = (
        pl.BlockSpec((1, rows, d_att), tile),
        pl.BlockSpec((1, rows, d_att), tile),
        pl.BlockSpec((1, rows, d_att), tile),
        pl.BlockSpec((1, rows, d_att), tile),
        pl.BlockSpec((1, rows, d_att), tile),
        pl.BlockSpec((1, rows, d_rec), tile),
        pl.BlockSpec((1, rows, d_att), tile),
        pl.BlockSpec((1, 1, d_rec), per_b),
        pl.BlockSpec((1, SUBLANES, d_rec), per_b),
    )
    in_specs = [
        pl.BlockSpec((1, rows, d), tile),
        pl.BlockSpec((1, 1, d), per_b),
        pl.BlockSpec((1, 1, d), per_b),
        pl.BlockSpec((d, d_in), const2),
        pl.BlockSpec((CONV_W, d_rec), const2),
        pl.BlockSpec((1, d_rec), const2),
        pl.BlockSpec((d_rec, 2 * d_rec), const2),
        pl.BlockSpec((1, 2 * d_rec), const2),
        pl.BlockSpec((1, d_rec), const2),
        pl.BlockSpec((1, SUBLANES, d_rec), per_b),
        pl.BlockSpec((1, 1, d_rec), per_b),
    ]
    return pl.pallas_call(
        kern,
        grid=(bsz, nt),
        in_specs=in_specs,
        out_specs=out_specs,
        out_shape=out_shapes,
        scratch_shapes=[
            pltpu.VMEM((SUBLANES + rows, d_rec), F32),
            pltpu.VMEM((SUBLANES, d_rec), F32),
            pltpu.VMEM((rows, d_rec), F32),
            pltpu.VMEM((rows, d_rec), F32),
            pltpu.VMEM((rows, d_rec), F32),
        ],
        compiler_params=_params(2),
        name="pre_mixer",
    )(x, shift, scale, win, cw, cb, wg, bg, lam, hist0, h0)


def _attn_kernel(q_ref, k_ref, v_ref, bcol_ref, u_ref, o_ref, acc_scr, car_scr,
                 z0_scr, z1_scr, sp0_scr, sp1_scr, w0_scr, w1_scr, *, blk, pairs):
    qi = pl.program_id(2)
    heads = 2 * pairs
    z_scr, sp_scr, w_scr = (z0_scr, z1_scr), (sp0_scr, sp1_scr), (w0_scr, w1_scr)
    lane = lax.broadcasted_iota(jnp.int32, (blk, LANES), 1)
    row = lax.broadcasted_iota(jnp.int32, (blk, blk), 0)
    col = lax.broadcasted_iota(jnp.int32, (blk, blk), 1)
    causal = col < row
    u = u_ref[...]
    dn_nt = (((1,), (1,)), ((), ()))

    q_aug = []
    for h in range(heads):
        p, s = divmod(h, 2)
        qp = q_ref[0, :, p * LANES:(p + 1) * LANES]
        in_head = (lane >= HEAD_DIM) if s else (lane < HEAD_DIM)
        qh = jnp.where(in_head, qp, jnp.zeros_like(qp))
        sel = jnp.where((lane == 2 * s) | (lane == 2 * s + 1), 1.0, 0.0).astype(BF16)
        q_aug.append(jnp.concatenate([qh, sel], axis=1))

    acc_scr[...] = jnp.zeros_like(acc_scr)
    car_scr[...] = jnp.zeros_like(car_scr)

    def rows_of(kb):
        return pl.ds(pl.multiple_of(kb * blk, blk), blk)

    def stage1(kb, slot, masked):
        for h in range(heads):
            p = h // 2
            k_aug = jnp.concatenate([k_ref[0, rows_of(kb), p * LANES:(p + 1) * LANES], bcol_ref[p]], axis=1)
            z2 = lax.dot_general(q_aug[h], k_aug, dn_nt, preferred_element_type=F32)
            z2 = jnp.minimum(z2, Z2_MAX)
            sp = jnp.log(1.0 + jnp.exp2(z2)) * LOG2E
            if masked:
                sp = jnp.where(causal, sp, 0.0)
            z_scr[slot][h] = z2
            sp_scr[slot][h] = sp.astype(BF16)

    def stage2(slot, masked, keep=None):
        for h in range(heads):
            cs = jnp.dot(sp_scr[slot][h], u, preferred_element_type=F32)
            cs = cs + jnp.concatenate([car_scr[h]] * (blk // LANES), axis=1)
            w = jnp.exp2(cs + z_scr[slot][h])
            if masked:
                w = jnp.where(causal, w, 0.0)
            if keep is not None:
                w = jnp.where(keep, w, 0.0)
            car_scr[h] = jnp.broadcast_to(cs[:, 0:1], (blk, LANES))
            w_scr[slot][h] = w.astype(BF16)

    def stage3(slot, kb):
        for h in range(heads):
            p = h // 2
            acc_scr[h] += jnp.dot(w_scr[slot][h], v_ref[0, rows_of(kb), p * LANES:(p + 1) * LANES],
                                  preferred_element_type=F32)

    def step(t, slot):
        kb = qi - t
        stage3(slot, kb + 2)
        stage2(1 - slot, False)
        stage1(kb, slot, False)

    stage1(qi, 0, True)
    stage2(0, True)
    stage1(jnp.maximum(qi - 1, 0), 1, False)

    def steps_from(t0, count):
        for i in range(count):
            step(t0 + i, i % 2)

    n_steps = jnp.maximum(qi - 1, 0)
    n_long = n_steps // ATT_UNROLL

    def long_body(n, c):
        steps_from(2 + ATT_UNROLL * n, ATT_UNROLL)
        return c

    def pair_body(n, c):
        steps_from(2 + ATT_UNROLL * n_long + 2 * n, 2)
        return c

    lax.fori_loop(0, n_long, long_body, 0)
    lax.fori_loop(0, (n_steps - ATT_UNROLL * n_long) // 2, pair_body, 0)
    odd = jnp.logical_and(qi >= 2, qi % 2 == 0)

    @pl.when(odd)
    def _():
        step(qi, 0)
        stage3(1, 1)
        stage2(0, False)
        stage3(0, 0)

    @pl.when(jnp.logical_not(odd))
    def _():
        stage3(0, jnp.minimum(qi, 1))
        stage2(1, False, keep=qi >= 1)
        stage3(1, 0)

    for p in range(pairs):
        o_ref[0, :, p * LANES:(p + 1) * LANES] = jnp.where(lane < HEAD_DIM, acc_scr[2 * p], acc_scr[2 * p + 1])


def _attn_call(qa, ka, va, bcol, u, *, blk, pairs):
    bsz, seq, d_att = qa.shape
    width = pairs * LANES
    n_groups = d_att // width
    kern = functools.partial(_attn_kernel, blk=blk, pairs=pairs)
    return pl.pallas_call(
        kern,
        grid=(bsz, n_groups, seq // blk),
        in_specs=[
            pl.BlockSpec((1, blk, width), lambda b, g, i: (b, i, g)),
            pl.BlockSpec((1, seq, width), lambda b, g, i: (b, 0, g)),
            pl.BlockSpec((1, seq, width), lambda b, g, i: (b, 0, g)),
            pl.BlockSpec((pairs, blk, LANES), lambda b, g, i: (g, 0, 0)),
            pl.BlockSpec((blk, blk), lambda b, g, i: (0, 0)),
        ],
        out_specs=pl.BlockSpec((1, blk, width), lambda b, g, i: (b, i, g)),
        out_shape=jax.ShapeDtypeStruct((bsz, seq, d_att), F32),
        scratch_shapes=[
            pltpu.VMEM((2 * pairs, blk, LANES), F32),
            pltpu.VMEM((2 * pairs, blk, LANES), F32),
            pltpu.VMEM((2 * pairs, blk, blk), F32),
            pltpu.VMEM((2 * pairs, blk, blk), F32),
            pltpu.VMEM((2 * pairs, blk, blk), BF16),
            pltpu.VMEM((2 * pairs, blk, blk), BF16),
            pltpu.VMEM((2 * pairs, blk, blk), BF16),
            pltpu.VMEM((2 * pairs, blk, blk), BF16),
        ],
        compiler_params=_params(3),
        name="prompt_attention",
    )(qa, ka, va, bcol, u)


def _bias_columns(sb_bias_l, blk):
    nb = sb_bias_l.astype(F32) * LOG2E
    hi = nb.astype(BF16)
    lo = (nb - hi.astype(F32)).astype(BF16)
    cols = jnp.stack([hi[0::2], lo[0::2], hi[1::2], lo[1::2]], axis=1)
    cols = jnp.pad(cols, ((0, 0), (0, LANES - 4)))
    return jnp.broadcast_to(cols[:, None, :], (cols.shape[0], blk, LANES))


def _same_head(r, c, t_new):
    assert t_new & (t_new - 1) == 0 and HEAD_DIM & (HEAD_DIM - 1) == 0
    return jnp.right_shift(r, t_new.bit_length() - 1) == jnp.right_shift(c, HEAD_DIM.bit_length() - 1)


def _sattn_kernel(pt_ref, q_ref, kn_ref, vn_ref, bias_ref, us_ref, *rest, n_pages_step, t_new, d_att):
    k_refs = rest[:n_pages_step]
    v_refs = rest[n_pages_step:2 * n_pages_step]
    o_ref = rest[2 * n_pages_step]
    qbd_scr, acc_scr, car_scr, z_scr, sp_scr, w_scr = rest[2 * n_pages_step + 1:]
    j = pl.program_id(1)
    nj = pl.num_programs(1)
    rows = N_HEADS * t_new
    bias = bias_ref[...]
    dn_nt = (((1,), (1,)), ((), ()))

    def softplus2(z2):
        return jnp.log(1.0 + jnp.exp2(z2)) * LOG2E

    @pl.when(j == 0)
    def _():
        q = q_ref[0]
        qt = jnp.concatenate([q] * N_HEADS, axis=0)
        r = lax.broadcasted_iota(jnp.int32, (rows, d_att), 0)
        c = lax.broadcasted_iota(jnp.int32, (rows, d_att), 1)
        qbd = jnp.where(_same_head(r, c, t_new), qt, 0.0).astype(BF16)
        qbd_scr[...] = qbd
        pad = jnp.zeros((PAGE_SIZE - t_new, d_att), F32)
        kn = jnp.concatenate([kn_ref[0], pad], axis=0).astype(BF16)
        vn = jnp.concatenate([vn_ref[0], pad], axis=0).astype(BF16)
        r1 = lax.broadcasted_iota(jnp.int32, (rows, PAGE_SIZE), 0)
        c1 = lax.broadcasted_iota(jnp.int32, (rows, PAGE_SIZE), 1)
        visible = c1 < jnp.bitwise_and(r1, t_new - 1)
        z2 = jnp.minimum(lax.dot_general(qbd, kn, dn_nt, preferred_element_type=F32) + bias, Z2_MAX)
        sp = jnp.where(visible, softplus2(z2), 0.0)
        cs = jnp.dot(sp.astype(BF16), us_ref[...], preferred_element_type=F32)
        w = jnp.where(visible, jnp.exp2(cs[:, :PAGE_SIZE] + z2), 0.0)
        acc_scr[...] = jnp.dot(w.astype(BF16), vn, preferred_element_type=F32)
        car_scr[...] = cs[:, PAGE_SIZE:]

    def scores(p):
        kt = k_refs[p][...].reshape(d_att, PAGE_SIZE).astype(BF16)
        z2 = jnp.minimum(jnp.dot(qbd, kt, preferred_element_type=F32) + bias, Z2_MAX)
        z_scr[p] = z2
        sp_scr[p] = softplus2(z2).astype(BF16)

    def weights(p, run):
        cs = jnp.dot(sp_scr[p], us, preferred_element_type=F32)
        w_scr[p] = jnp.exp2(cs[:, :PAGE_SIZE] + run + z_scr[p]).astype(BF16)
        return run + cs[:, PAGE_SIZE:]

    def values(p):
        vt = v_refs[p][...].reshape(d_att, PAGE_SIZE).astype(BF16)
        return lax.dot_general(w_scr[p], vt, dn_nt, preferred_element_type=F32)

    qbd = qbd_scr[...]
    us = us_ref[...]
    run = car_scr[...]
    acc = acc_scr[...]
    for s in range(n_pages_step + 2):
        p = n_pages_step - 1 - s
        if p + 2 < n_pages_step:
            acc = acc + values(p + 2)
        if 0 <= p + 1 < n_pages_step:
            run = weights(p + 1, run)
        if p >= 0:
            scores(p)
    car_scr[...] = run
    acc_scr[...] = acc

    @pl.when(j == nj - 1)
    def _():
        r = lax.broadcasted_iota(jnp.int32, (rows, d_att), 0)
        c = lax.broadcasted_iota(jnp.int32, (rows, d_att), 1)
        sel = jnp.where(_same_head(r, c, t_new), acc, 0.0)
        out = sel[0:t_new, :]
        for h in range(1, N_HEADS):
            out = out + sel[h * t_new:(h + 1) * t_new, :]
        o_ref[0] = out


def _suffix_and_total(n):
    return jnp.concatenate([-_suffix_ones(n), -jnp.ones((n, n), BF16)], axis=1)


def _sattn_call(page_table, q, k_new, v_new, bias_rows, cache_kt, cache_vt, *, layer):
    bsz, t_new, d_att = q.shape
    n_pages = page_table.shape[1]
    pps = PAGES_PER_STEP
    n_steps = n_pages // pps
    rows = N_HEADS * t_new

    def page_map(p):
        return lambda b, j, pt: (layer, pt[b, (n_steps - 1 - j) * pps + p], 0, 0, 0)

    per_b = lambda b, j, pt: (b, 0, 0)
    const2 = lambda b, j, pt: (0, 0)
    page_specs = [pl.BlockSpec((None, None, N_HEADS, HEAD_DIM, PAGE_SIZE), page_map(p)) for p in range(pps)]
    kern = functools.partial(_sattn_kernel, n_pages_step=pps, t_new=t_new, d_att=d_att)
    return pl.pallas_call(
        kern,
        grid_spec=pltpu.PrefetchScalarGridSpec(
            num_scalar_prefetch=1,
            grid=(bsz, n_steps),
            in_specs=[
                pl.BlockSpec((1, t_new, d_att), per_b),
                pl.BlockSpec((1, t_new, d_att), per_b),
                pl.BlockSpec((1, t_new, d_att), per_b),
                pl.BlockSpec((rows, PAGE_SIZE), const2),
                pl.BlockSpec((PAGE_SIZE, 2 * PAGE_SIZE), const2),
            ] + page_specs + page_specs,
            out_specs=pl.BlockSpec((1, t_new, d_att), per_b),
            scratch_shapes=[
                pltpu.VMEM((rows, d_att), BF16),
                pltpu.VMEM((rows, d_att), F32),
                pltpu.VMEM((rows, PAGE_SIZE), F32),
                pltpu.VMEM((pps, rows, PAGE_SIZE), F32),
                pltpu.VMEM((pps, rows, PAGE_SIZE), BF16),
                pltpu.VMEM((pps, rows, PAGE_SIZE), BF16),
            ],
        ),
        out_shape=jax.ShapeDtypeStruct((bsz, t_new, d_att), F32),
        compiler_params=_params(2),
        name="sample_attention",
    )(page_table, q, k_new, v_new, bias_rows, _suffix_and_total(PAGE_SIZE),
      *([cache_kt] * pps), *([cache_vt] * pps))


def _post_kernel(x_ref, yrec_ref, o_ref, sgb_ref, gate_ref, wout_ref, lng_ref, lnb_ref, out_ref,
                 *, d_rec, alpha):
    yatt = (o_ref[0] * sgb_ref[0]).astype(BF16)
    y = jnp.dot(yrec_ref[0], wout_ref[0:d_rec, :], preferred_element_type=F32)
    y = y + jnp.dot(yatt, wout_ref[d_rec:, :], preferred_element_type=F32)
    y = y * gate_ref[0]
    s = alpha * x_ref[0] + y
    out_ref[0] = _layer_norm(s) * lng_ref[...] + lnb_ref[...]


def _post_call(x, yrec, o, sgb, gate, wout, lng, lnb, *, rows, alpha):
    bsz, seq, d = x.shape
    d_rec = yrec.shape[2]
    d_att = o.shape[2]
    kern = functools.partial(_post_kernel, d_rec=d_rec, alpha=alpha)
    tile = lambda b, t: (b, t, 0)
    const2 = lambda b, t: (0, 0)
    return pl.pallas_call(
        kern,
        grid=(bsz, seq // rows),
        in_specs=[
            pl.BlockSpec((1, rows, d), tile),
            pl.BlockSpec((1, rows, d_rec), tile),
            pl.BlockSpec((1, rows, d_att), tile),
            pl.BlockSpec((1, rows, d_att), tile),
            pl.BlockSpec((1, 1, d), lambda b, t: (b, 0, 0)),
            pl.BlockSpec((d_rec + d_att, d), const2),
            pl.BlockSpec((1, d), const2),
            pl.BlockSpec((1, d), const2),
        ],
        out_specs=pl.BlockSpec((1, rows, d), tile),
        out_shape=jax.ShapeDtypeStruct((bsz, seq, d), F32),
        compiler_params=_params(2),
        name="post_mixer",
    )(x, yrec, o, sgb, gate, wout, lng, lnb)


def _suffix_ones(n):
    s = lax.broadcasted_iota(jnp.int32, (n, n), 0)
    j = lax.broadcasted_iota(jnp.int32, (n, n), 1)
    return (s >= j).astype(BF16)


def _block_diag(w):
    n, d, e = w.shape
    eye = jnp.eye(n, dtype=w.dtype)
    return (eye[:, None, :, None] * w[:, :, None, :]).reshape(n * d, n * e)


def kernel(x_prompt, x_sample, cache_k, cache_v, state_h, state_conv, page_table, c_prompt, c_sample,
           w_ada, b_ada, w_in, conv_w, conv_b, w_r, b_r, w_i, b_i, lam, sb_bias, w_out, ln_g, ln_b):
    depth = w_ada.shape[0]
    bp, seq, d = x_prompt.shape
    bs, t_new, _ = x_sample.shape
    d_rec = conv_w.shape[2]
    d_att = N_HEADS * HEAD_DIM
    alpha = (2 * depth) ** 0.25

    n_c = bp + bs
    pad_rows = (-n_c) % SUBLANES
    c_all = jnp.concatenate([c_prompt, c_sample, jnp.zeros((pad_rows, d), F32)], axis=0)
    mod = _modulation(c_all, w_ada, b_ada)

    u_prompt = -_suffix_ones(ATT_BLOCK)
    cache_kt = jnp.transpose(cache_k, (0, 1, 3, 4, 2))
    cache_vt = jnp.transpose(cache_v, (0, 1, 3, 4, 2))
    zeros_hist = jnp.zeros((bp, SUBLANES, d_rec), F32)
    zeros_h = jnp.zeros((bp, 1, d_rec), F32)
    hist_pad = jnp.zeros((depth, bs, SUBLANES - (CONV_W - 1), d_rec), F32)
    state_conv8 = jnp.concatenate([hist_pad, state_conv], axis=2)

    xp, xs = x_prompt, x_sample
    outs = [[] for _ in range(8)]
    for l in range(depth):
        shift, scale, gate = mod[l, :, :d], mod[l, :, d:2 * d], mod[l, :, 2 * d:]
        win = w_in[l].astype(BF16)
        wg = jnp.concatenate([_block_diag(w_r[l]), _block_diag(w_i[l])], axis=1).astype(BF16)
        bg = jnp.concatenate([b_r[l], b_i[l]])[None, :]
        wout = w_out[l].astype(BF16)
        weights = (win, conv_w[l], conv_b[l][None, :], wg, bg, lam[l][None, :])
        lng, lnb = ln_g[l][None, :], ln_b[l][None, :]

        sl = slice(0, bp)
        k, v, qa, ka, va, yrec, sgb, hl, cv = _pre_call(
            xp, shift[sl, None, :], scale[sl, None, :], *weights, zeros_hist, zeros_h,
            rows=PRE_ROWS, reset_first=True, q_dtype=BF16, q_scale=LOG2E / math.sqrt(HEAD_DIM))
        o = _attn_call(qa, ka, va, _bias_columns(sb_bias[l], ATT_BLOCK), u_prompt, blk=ATT_BLOCK, pairs=ATT_PAIRS)
        xp = _post_call(xp, yrec, o, sgb, gate[sl, None, :], wout, lng, lnb, rows=POST_ROWS, alpha=alpha)
        outs[0].append(k.reshape(bp, seq, N_HEADS, HEAD_DIM))
        outs[1].append(v.reshape(bp, seq, N_HEADS, HEAD_DIM))
        outs[2].append(hl[:, 0, :])
        outs[3].append(cv[:, SUBLANES - (CONV_W - 1):, :])

        sl = slice(bp, bp + bs)
        k, v, qa, _, _, yrec, sgb, hl, cv = _pre_call(
            xs, shift[sl, None, :], scale[sl, None, :], *weights, state_conv8[l], state_h[l][:, None, :],
            rows=t_new, reset_first=False, q_dtype=F32, q_scale=LOG2E / math.sqrt(HEAD_DIM))
        bias_rows = jnp.broadcast_to(jnp.repeat(sb_bias[l] * LOG2E, t_new)[:, None], (N_HEADS * t_new, LANES))
        o = _sattn_call(page_table, qa, k, v, bias_rows, cache_kt, cache_vt, layer=l)
        xs = _post_call(xs, yrec, o, sgb, gate[sl, None, :], wout, lng, lnb, rows=t_new, alpha=alpha)
        outs[4].append(k.reshape(bs, t_new, N_HEADS, HEAD_DIM))
        outs[5].append(v.reshape(bs, t_new, N_HEADS, HEAD_DIM))
        outs[6].append(hl[:, 0, :])
        outs[7].append(cv[:, SUBLANES - (CONV_W - 1):, :])

    stacked = [jnp.stack(o_) for o_ in outs]
    return (xp, xs, *stacked)
```

```python
import functools
import math

import jax
import jax.numpy as jnp
from jax import lax
from jax.experimental import pallas as pl
from jax.experimental.pallas import tpu as pltpu

F32 = jnp.float32
BF16 = jnp.bfloat16

N_HEADS = 8
HEAD_DIM = 64
N_REC_BLOCKS = 8
CONV_W = 4
RG_C = 8.0
PAGE_SIZE = 128
LN_EPS = 1e-5
LOG2E = math.log2(math.e)
Z2_MAX = 120.0

SUBLANES = 8
LANES = 128
VMEM_LIMIT = 56 * 1024 * 1024

PRE_ROWS = 512
POST_ROWS = 512
ATT_BLOCK = 256
ATT_PAIRS = 2
ATT_UNROLL = 8
PAGES_PER_STEP = 32


def _sigmoid(x):
    return 1.0 / (1.0 + jnp.exp(-x))


def _layer_norm(x):
    mu = jnp.mean(x, axis=-1, keepdims=True)
    xc = x - mu
    var = jnp.mean(xc * xc, axis=-1, keepdims=True)
    return xc * lax.rsqrt(var + LN_EPS)


def _params(n_axes):
    return pltpu.CompilerParams(dimension_semantics=("arbitrary",) * n_axes,
                                vmem_limit_bytes=VMEM_LIMIT)


def _mod_kernel(c_ref, w_ref, b_ref, o_ref):
    c = c_ref[...]
    sc = c * _sigmoid(c)
    o_ref[0] = jnp.dot(sc, w_ref[0], precision=lax.Precision.HIGHEST,
                       preferred_element_type=F32) + b_ref[0]


def _modulation(c_all, w_ada, b_ada):
    depth, d, d3 = w_ada.shape
    rows = c_all.shape[0]
    n_col = d3 // d
    return pl.pallas_call(
        _mod_kernel,
        grid=(depth, n_col),
        in_specs=[
            pl.BlockSpec((rows, d), lambda l, n: (0, 0)),
            pl.BlockSpec((1, d, d), lambda l, n: (l, 0, n)),
            pl.BlockSpec((1, 1, d), lambda l, n: (l, 0, n)),
        ],
        out_specs=pl.BlockSpec((1, rows, d), lambda l, n: (l, 0, n)),
        out_shape=jax.ShapeDtypeStruct((depth, rows, d3), F32),
        compiler_params=_params(2),
        name="adaln_modulation",
    )(c_all, w_ada, b_ada.reshape(depth, 1, d3))


def _pre_kernel(x_ref, shift_ref, scale_ref, win_ref, cw_ref, cb_ref, wg_ref, bg_ref, lam_ref,
                hist0_ref, h0_ref,
                k_ref, v_ref, qa_ref, ka_ref, va_ref, yrec_ref, sgb_ref, hlast_ref, conv_ref,
                xp_scr, h_scr, a_scr, b_scr, sga_scr, *, rows, d_rec, d_att, reset_first, q_scale):
    ti = pl.program_id(1)

    @pl.when(ti == 0)
    def _():
        xp_scr[0:SUBLANES, :] = hist0_ref[0]
        h_scr[...] = jnp.broadcast_to(h0_ref[0], (SUBLANES, d_rec))

    x = x_ref[0]
    u = _layer_norm(x) * (1.0 + scale_ref[0]) + shift_ref[0]
    z = jnp.dot(u.astype(BF16), win_ref[...], preferred_element_type=F32)
    o1, o2 = d_rec, 2 * d_rec
    o3, o4, o5 = o2 + d_att, o2 + 2 * d_att, o2 + 3 * d_att
    xa = z[:, :o1]
    ga = z[:, o1:o2]
    q = z[:, o2:o3]
    k = z[:, o3:o4]
    v = z[:, o4:o5]
    gb = z[:, o5:]
    k_ref[0] = k
    v_ref[0] = v
    qa_ref[0] = (q * q_scale).astype(qa_ref.dtype)
    ka_ref[0] = k.astype(BF16)
    va_ref[0] = v.astype(BF16)
    sgb_ref[0] = gb * _sigmoid(gb)
    sga_scr[...] = ga * _sigmoid(ga)

    xp_scr[SUBLANES:SUBLANES + rows, :] = xa
    cw = cw_ref[...]
    xc = cb_ref[...] + cw[3:4, :] * xa
    for j in range(CONV_W - 1):
        off = SUBLANES - (CONV_W - 1) + j
        xc = xc + cw[j:j + 1, :] * xp_scr[off:off + rows, :]
    last_rows = xa[rows - SUBLANES:, :]
    xp_scr[0:SUBLANES, :] = last_rows
    conv_ref[0] = last_rows

    gates = jnp.dot(xc.astype(BF16), wg_ref[...], preferred_element_type=F32) + bg_ref[...]
    r = _sigmoid(gates[:, :d_rec])
    i = _sigmoid(gates[:, d_rec:])
    lam = lam_ref[...]
    c_lam = -RG_C * (jnp.maximum(-lam, 0.0) + jnp.log1p(jnp.exp(-jnp.abs(lam))))
    a = jnp.exp(c_lam * r)
    mult = jnp.sqrt(1.0 - a * a)
    if reset_first:
        row = lax.broadcasted_iota(jnp.int32, (rows, d_rec), 0)
        mult = jnp.where(jnp.logical_and(row == 0, ti == 0), 1.0, mult)
    a_scr[...] = a
    b_scr[...] = mult * (i * xc)

    srow = lax.broadcasted_iota(jnp.int32, (SUBLANES, d_rec), 0)

    def group(g, hprev):
        sl = pl.ds(pl.multiple_of(g * SUBLANES, SUBLANES), SUBLANES)
        ag = a_scr[sl, :]
        bg = b_scr[sl, :]
        for s in (1, 2, 4):
            keep = srow >= s
            a_sh = jnp.where(keep, pltpu.roll(ag, s, 0), 1.0)
            b_sh = jnp.where(keep, pltpu.roll(bg, s, 0), 0.0)
            bg = ag * b_sh + bg
            ag = ag * a_sh
        h = ag * hprev + bg
        b_scr[sl, :] = h
        return jnp.broadcast_to(h[SUBLANES - 1:SUBLANES, :], (SUBLANES, d_rec))

    hfin = lax.fori_loop(0, rows // SUBLANES, group, h_scr[...])
    h_scr[...] = hfin
    hlast_ref[0] = hfin[0:1, :]
    yrec_ref[0] = (b_scr[...] * sga_scr[...]).astype(BF16)


def _pre_call(x, shift, scale, win, cw, cb, wg, bg, lam, hist0, h0, *, rows, reset_first, q_dtype, q_scale):
    bsz, seq, d = x.shape
    d_in = win.shape[1]
    d_rec = cw.shape[1]
    d_att = (d_in - 2 * d_rec) // 4
    nt = seq // rows
    kern = functools.partial(_pre_kernel, rows=rows, d_rec=d_rec, d_att=d_att, reset_first=reset_first,
                             q_scale=q_scale)
    const2 = lambda b, t: (0, 0)
    per_b = lambda b, t: (b, 0, 0)
    tile = lambda b, t: (b, t, 0)
    out_shapes = (
        jax.ShapeDtypeStruct((bsz, seq, d_att), F32),
        jax.ShapeDtypeStruct((bsz, seq, d_att), F32),
        jax.ShapeDtypeStruct((bsz, seq, d_att), q_dtype),
        jax.ShapeDtypeStruct((bsz, seq, d_att), BF16),
        jax.ShapeDtypeStruct((bsz, seq, d_att), BF16),
        jax.ShapeDtypeStruct((bsz, seq, d_rec), BF16),
        jax.ShapeDtypeStruct((bsz, seq, d_att), F32),
        jax.ShapeDtypeStruct((bsz, 1, d_rec), F32),
        jax.ShapeDtypeStruct((bsz, SUBLANES, d_rec), F32),
    )
    out_specs = (
        pl.BlockSpec((1, rows, d_att), tile),
        pl.BlockSpec((1, rows, d_att), tile),
        pl.BlockSpec((1, rows, d_att), tile),
        pl.BlockSpec((1, rows, d_att), tile),
        pl.BlockSpec((1, rows, d_att), tile),
        pl.BlockSpec((1, rows, d_rec), tile),
        pl.BlockSpec((1, rows, d_att), tile),
        pl.BlockSpec((1, 1, d_rec), per_b),
        pl.BlockSpec((1, SUBLANES, d_rec), per_b),
    )
    in_specs = [
        pl.BlockSpec((1, rows, d), tile),
        pl.BlockSpec((1, 1, d), per_b),
        pl.BlockSpec((1, 1, d), per_b),
        pl.BlockSpec((d, d_in), const2),
        pl.BlockSpec((CONV_W, d_rec), const2),
        pl.BlockSpec((1, d_rec), const2),
        pl.BlockSpec((d_rec, 2 * d_rec), const2),
        pl.BlockSpec((1, 2 * d_rec), const2),
        pl.BlockSpec((1, d_rec), const2),
        pl.BlockSpec((1, SUBLANES, d_rec), per_b),
        pl.BlockSpec((1, 1, d_rec), per_b),
    ]
    return pl.pallas_call(
        kern,
        grid=(bsz, nt),
        in_specs=in_specs,
        out_specs=out_specs,
        out_shape=out_shapes,
        scratch_shapes=[
            pltpu.VMEM((SUBLANES + rows, d_rec), F32),
            pltpu.VMEM((SUBLANES, d_rec), F32),
            pltpu.VMEM((rows, d_rec), F32),
            pltpu.VMEM((rows, d_rec), F32),
            pltpu.VMEM((rows, d_rec), F32),
        ],
        compiler_params=_params(2),
        name="pre_mixer",
    )(x, shift, scale, win, cw, cb, wg, bg, lam, hist0, h0)


def _attn_kernel(q_ref, k_ref, v_ref, bcol_ref, u_ref, o_ref, acc_scr, car_scr,
                 z0_scr, z1_scr, sp0_scr, sp1_scr, w0_scr, w1_scr, *, blk, pairs):
    qi = pl.program_id(2)
    heads = 2 * pairs
    z_scr, sp_scr, w_scr = (z0_scr, z1_scr), (sp0_scr, sp1_scr), (w0_scr, w1_scr)
    lane = lax.broadcasted_iota(jnp.int32, (blk, LANES), 1)
    row = lax.broadcasted_iota(jnp.int32, (blk, blk), 0)
    col = lax.broadcasted_iota(jnp.int32, (blk, blk), 1)
    causal = col < row
    u = u_ref[...]
    dn_nt = (((1,), (1,)), ((), ()))

    q_aug = []
    for h in range(heads):
        p, s = divmod(h, 2)
        qp = q_ref[0, :, p * LANES:(p + 1) * LANES]
        in_head = (lane >= HEAD_DIM) if s else (lane < HEAD_DIM)
        qh = jnp.where(in_head, qp, jnp.zeros_like(qp))
        sel = jnp.where((lane == 2 * s) | (lane == 2 * s + 1), 1.0, 0.0).astype(BF16)
        q_aug.append(jnp.concatenate([qh, sel], axis=1))

    acc_scr[...] = jnp.zeros_like(acc_scr)
    car_scr[...] = jnp.zeros_like(car_scr)

    def rows_of(kb):
        return pl.ds(pl.multiple_of(kb * blk, blk), blk)

    def stage1(kb, slot, masked):
        for h in range(heads):
            p = h // 2
            k_aug = jnp.concatenate([k_ref[0, rows_of(kb), p * LANES:(p + 1) * LANES], bcol_ref[p]], axis=1)
            z2 = lax.dot_general(q_aug[h], k_aug, dn_nt, preferred_element_type=F32)
            z2 = jnp.minimum(z2, Z2_MAX)
            sp = jnp.log(1.0 + jnp.exp2(z2)) * LOG2E
            if masked:
                sp = jnp.where(causal, sp, 0.0)
            z_scr[slot][h] = z2
            sp_scr[slot][h] = sp.astype(BF16)

    def stage2(slot, masked, keep=None):
        for h in range(heads):
            cs = jnp.dot(sp_scr[slot][h], u, preferred_element_type=F32)
            cs = cs + jnp.concatenate([car_scr[h]] * (blk // LANES), axis=1)
            w = jnp.exp2(cs + z_scr[slot][h])
            if masked:
                w = jnp.where(causal, w, 0.0)
            if keep is not None:
                w = jnp.where(keep, w, 0.0)
            car_scr[h] = jnp.broadcast_to(cs[:, 0:1], (blk, LANES))
            w_scr[slot][h] = w.astype(BF16)

    def stage3(slot, kb):
        for h in range(heads):
            p = h // 2
            acc_scr[h] += jnp.dot(w_scr[slot][h], v_ref[0, rows_of(kb), p * LANES:(p + 1) * LANES],
                                  preferred_element_type=F32)

    def step(t, slot):
        kb = qi - t
        stage3(slot, kb + 2)
        stage2(1 - slot, False)
        stage1(kb, slot, False)

    stage1(qi, 0, True)
    stage2(0, True)
    stage1(jnp.maximum(qi - 1, 0), 1, False)

    def steps_from(t0, count):
        for i in range(count):
            step(t0 + i, i % 2)

    n_steps = jnp.maximum(qi - 1, 0)
    n_long = n_steps // ATT_UNROLL

    def long_body(n, c):
        steps_from(2 + ATT_UNROLL * n, ATT_UNROLL)
        return c

    def pair_body(n, c):
        steps_from(2 + ATT_UNROLL * n_long + 2 * n, 2)
        return c

    lax.fori_loop(0, n_long, long_body, 0)
    lax.fori_loop(0, (n_steps - ATT_UNROLL * n_long) // 2, pair_body, 0)
    odd = jnp.logical_and(qi >= 2, qi % 2 == 0)

    @pl.when(odd)
    def _():
        step(qi, 0)
        stage3(1, 1)
        stage2(0, False)
        stage3(0, 0)

    @pl.when(jnp.logical_not(odd))
    def _():
        stage3(0, jnp.minimum(qi, 1))
        stage2(1, False, keep=qi >= 1)
        stage3(1, 0)

    for p in range(pairs):
        o_ref[0, :, p * LANES:(p + 1) * LANES] = jnp.where(lane < HEAD_DIM, acc_scr[2 * p], acc_scr[2 * p + 1])


def _attn_call(qa, ka, va, bcol, u, *, blk, pairs):
    bsz, seq, d_att = qa.shape
    width = pairs * LANES
    n_groups = d_att // width
    kern = functools.partial(_attn_kernel, blk=blk, pairs=pairs)
    return pl.pallas_call(
        kern,
        grid=(bsz, n_groups, seq // blk),
        in_specs=[
            pl.BlockSpec((1, blk, width), lambda b, g, i: (b, i, g)),
            pl.BlockSpec((1, seq, width), lambda b, g, i: (b, 0, g)),
            pl.BlockSpec((1, seq, width), lambda b, g, i: (b, 0, g)),
            pl.BlockSpec((pairs, blk, LANES), lambda b, g, i: (g, 0, 0)),
            pl.BlockSpec((blk, blk), lambda b, g, i: (0, 0)),
        ],
        out_specs=pl.BlockSpec((1, blk, width), lambda b, g, i: (b, i, g)),
        out_shape=jax.ShapeDtypeStruct((bsz, seq, d_att), F32),
        scratch_shapes=[
            pltpu.VMEM((2 * pairs, blk, LANES), F32),
            pltpu.VMEM((2 * pairs, blk, LANES), F32),
            pltpu.VMEM((2 * pairs, blk, blk), F32),
            pltpu.VMEM((2 * pairs, blk, blk), F32),
            pltpu.VMEM((2 * pairs, blk, blk), BF16),
            pltpu.VMEM((2 * pairs, blk, blk), BF16),
            pltpu.VMEM((2 * pairs, blk, blk), BF16),
            pltpu.VMEM((2 * pairs, blk, blk), BF16),
        ],
        compiler_params=_params(3),
        name="prompt_attention",
    )(qa, ka, va, bcol, u)


def _attn_kernel_t(q_ref, k_ref, vt_ref, bcol_ref, ut_ref, o_ref, acc_scr, car_scr,
                   z0_scr, z1_scr, sp0_scr, sp1_scr, w0_scr, w1_scr, *, blk, pairs):
    qi = pl.program_id(2)
    heads = 2 * pairs
    z_scr, sp_scr, w_scr = (z0_scr, z1_scr), (sp0_scr, sp1_scr), (w0_scr, w1_scr)
    lane = lax.broadcasted_iota(jnp.int32, (blk, LANES), 1)
    krow = lax.broadcasted_iota(jnp.int32, (blk, blk), 0)
    qcol = lax.broadcasted_iota(jnp.int32, (blk, blk), 1)
    causal = krow < qcol
    ut = ut_ref[...]
    dn_nt = (((1,), (1,)), ((), ()))

    q_aug = []
    for h in range(heads):
        p, s = divmod(h, 2)
        qp = q_ref[0, :, p * LANES:(p + 1) * LANES]
        in_head = (lane >= HEAD_DIM) if s else (lane < HEAD_DIM)
        qh = jnp.where(in_head, qp, jnp.zeros_like(qp))
        sel = jnp.where((lane == 2 * s) | (lane == 2 * s + 1), 1.0, 0.0).astype(BF16)
        q_aug.append(jnp.concatenate([qh, sel], axis=1))

    acc_scr[...] = jnp.zeros_like(acc_scr)
    car_scr[...] = jnp.zeros_like(car_scr)

    def rows_of(kb):
        return pl.ds(pl.multiple_of(kb * blk, blk), blk)

    def stage1(kb, slot, masked):
        for h in range(heads):
            p = h // 2
            k_aug = jnp.concatenate([k_ref[0, rows_of(kb), p * LANES:(p + 1) * LANES], bcol_ref[p]], axis=1)
            z2 = lax.dot_general(k_aug, q_aug[h], dn_nt, preferred_element_type=F32)
            z2 = jnp.minimum(z2, Z2_MAX)
            sp = jnp.log(1.0 + jnp.exp2(z2)) * LOG2E
            if masked:
                sp = jnp.where(causal, sp, 0.0)
            z_scr[slot][h] = z2
            sp_scr[slot][h] = sp.astype(BF16)

    def stage2(slot, masked, keep=None):
        for h in range(heads):
            cs = jnp.dot(ut, sp_scr[slot][h], preferred_element_type=F32)
            cs = cs + jnp.concatenate([car_scr[h]] * (blk // SUBLANES), axis=0)
            w = jnp.exp2(cs + z_scr[slot][h])
            if masked:
                w = jnp.where(causal, w, 0.0)
            if keep is not None:
                w = jnp.where(keep, w, 0.0)
            car_scr[h] = jnp.broadcast_to(cs[0:1, :], (SUBLANES, blk))
            w_scr[slot][h] = w.astype(BF16)

    def stage3(slot, kb):
        for h in range(heads):
            p = h // 2
            acc_scr[h] += jnp.dot(vt_ref[0, kb, p * LANES:(p + 1) * LANES, :], w_scr[slot][h],
                                  preferred_element_type=F32)

    def step(t, slot):
        kb = qi - t
        stage3(slot, kb + 2)
        stage2(1 - slot, False)
        stage1(kb, slot, False)

    stage1(qi, 0, True)
    stage2(0, True)
    stage1(jnp.maximum(qi - 1, 0), 1, False)

    def steps_from(t0, count):
        for i in range(count):
            step(t0 + i, i % 2)

    n_steps = jnp.maximum(qi - 1, 0)
    n_long = n_steps // ATT_UNROLL

    def long_body(n, c):
        steps_from(2 + ATT_UNROLL * n, ATT_UNROLL)
        return c

    def pair_body(n, c):
        steps_from(2 + ATT_UNROLL * n_long + 2 * n, 2)
        return c

    lax.fori_loop(0, n_long, long_body, 0)
    lax.fori_loop(0, (n_steps - ATT_UNROLL * n_long) // 2, pair_body, 0)
    odd = jnp.logical_and(qi >= 2, qi % 2 == 0)

    @pl.when(odd)
    def _():
        step(qi, 0)
        stage3(1, 1)
        stage2(0, False)
        stage3(0, 0)

    @pl.when(jnp.logical_not(odd))
    def _():
        stage3(0, jnp.minimum(qi, 1))
        stage2(1, False, keep=qi >= 1)
        stage3(1, 0)

    drow = lax.broadcasted_iota(jnp.int32, (LANES, blk), 0)
    for p in range(pairs):
        o_t = jnp.where(drow < HEAD_DIM, acc_scr[2 * p], acc_scr[2 * p + 1])
        o_ref[0, :, p * LANES:(p + 1) * LANES] = o_t.T


def _attn_call_t(qa, ka, va, bcol, u, *, blk, pairs):
    bsz, seq, d_att = qa.shape
    width = pairs * LANES
    n_groups = d_att // width
    n_kb = seq // blk
    vt = jnp.transpose(va.reshape(bsz, n_kb, blk, d_att), (0, 1, 3, 2))
    kern = functools.partial(_attn_kernel_t, blk=blk, pairs=pairs)
    return pl.pallas_call(
        kern,
        grid=(bsz, n_groups, seq // blk),
        in_specs=[
            pl.BlockSpec((1, blk, width), lambda b, g, i: (b, i, g)),
            pl.BlockSpec((1, seq, width), lambda b, g, i: (b, 0, g)),
            pl.BlockSpec((1, n_kb, width, blk), lambda b, g, i: (b, 0, g, 0)),
            pl.BlockSpec((pairs, blk, LANES), lambda b, g, i: (g, 0, 0)),
            pl.BlockSpec((blk, blk), lambda b, g, i: (0, 0)),
        ],
        out_specs=pl.BlockSpec((1, blk, width), lambda b, g, i: (b, i, g)),
        out_shape=jax.ShapeDtypeStruct((bsz, seq, d_att), F32),
        scratch_shapes=[
            pltpu.VMEM((2 * pairs, LANES, blk), F32),
            pltpu.VMEM((2 * pairs, SUBLANES, blk), F32),
            pltpu.VMEM((2 * pairs, blk, blk), F32),
            pltpu.VMEM((2 * pairs, blk, blk), F32),
            pltpu.VMEM((2 * pairs, blk, blk), BF16),
            pltpu.VMEM((2 * pairs, blk, blk), BF16),
            pltpu.VMEM((2 * pairs, blk, blk), BF16),
            pltpu.VMEM((2 * pairs, blk, blk), BF16),
        ],
        compiler_params=_params(3),
        name="prompt_attention",
    )(qa, ka, vt, bcol, u.T)


def _bias_columns(sb_bias_l, blk):
    nb = sb_bias_l.astype(F32) * LOG2E
    hi = nb.astype(BF16)
    lo = (nb - hi.astype(F32)).astype(BF16)
    cols = jnp.stack([hi[0::2], lo[0::2], hi[1::2], lo[1::2]], axis=1)
    cols = jnp.pad(cols, ((0, 0), (0, LANES - 4)))
    return jnp.broadcast_to(cols[:, None, :], (cols.shape[0], blk, LANES))


def _same_head(r, c, t_new):
    assert t_new & (t_new - 1) == 0 and HEAD_DIM & (HEAD_DIM - 1) == 0
    return jnp.right_shift(r, t_new.bit_length() - 1) == jnp.right_shift(c, HEAD_DIM.bit_length() - 1)


def _sattn_kernel(pt_ref, q_ref, kn_ref, vn_ref, bias_ref, us_ref, *rest, n_pages_step, t_new, d_att):
    k_refs = rest[:n_pages_step]
    v_refs = rest[n_pages_step:2 * n_pages_step]
    o_ref = rest[2 * n_pages_step]
    qbd_scr, acc_scr, car_scr, z_scr, sp_scr, w_scr = rest[2 * n_pages_step + 1:]
    j = pl.program_id(1)
    nj = pl.num_programs(1)
    rows = N_HEADS * t_new
    bias = bias_ref[...]
    dn_nt = (((1,), (1,)), ((), ()))

    def softplus2(z2):
        return jnp.log(1.0 + jnp.exp2(z2)) * LOG2E

    @pl.when(j == 0)
    def _():
        q = q_ref[0]
        qt = jnp.concatenate([q] * N_HEADS, axis=0)
        r = lax.broadcasted_iota(jnp.int32, (rows, d_att), 0)
        c = lax.broadcasted_iota(jnp.int32, (rows, d_att), 1)
        qbd = jnp.where(_same_head(r, c, t_new), qt, 0.0).astype(BF16)
        qbd_scr[...] = qbd
        pad = jnp.zeros((PAGE_SIZE - t_new, d_att), F32)
        kn = jnp.concatenate([kn_ref[0], pad], axis=0).astype(BF16)
        vn = jnp.concatenate([vn_ref[0], pad], axis=0).astype(BF16)
        r1 = lax.broadcasted_iota(jnp.int32, (rows, PAGE_SIZE), 0)
        c1 = lax.broadcasted_iota(jnp.int32, (rows, PAGE_SIZE), 1)
        visible = c1 < jnp.bitwise_and(r1, t_new - 1)
        z2 = jnp.minimum(lax.dot_general(qbd, kn, dn_nt, preferred_element_type=F32) + bias, Z2_MAX)
        sp = jnp.where(visible, softplus2(z2), 0.0)
        cs = jnp.dot(sp.astype(BF16), us_ref[...], preferred_element_type=F32)
        w = jnp.where(visible, jnp.exp2(cs[:, :PAGE_SIZE] + z2), 0.0)
        acc_scr[...] = jnp.dot(w.astype(BF16), vn, preferred_element_type=F32)
        car_scr[...] = cs[:, PAGE_SIZE:]

    def scores(p):
        kt = k_refs[p][...].reshape(d_att, PAGE_SIZE).astype(BF16)
        z2 = jnp.minimum(jnp.dot(qbd, kt, preferred_element_type=F32) + bias, Z2_MAX)
        z_scr[p] = z2
        sp_scr[p] = softplus2(z2).astype(BF16)

    def weights(p, run):
        cs = jnp.dot(sp_scr[p], us, preferred_element_type=F32)
        w_scr[p] = jnp.exp2(cs[:, :PAGE_SIZE] + run + z_scr[p]).astype(BF16)
        return run + cs[:, PAGE_SIZE:]

    def values(p):
        vt = v_refs[p][...].reshape(d_att, PAGE_SIZE).astype(BF16)
        return lax.dot_general(w_scr[p], vt, dn_nt, preferred_element_type=F32)

    qbd = qbd_scr[...]
    us = us_ref[...]
    run = car_scr[...]
    acc = acc_scr[...]
    for s in range(n_pages_step + 2):
        p = n_pages_step - 1 - s
        if p + 2 < n_pages_step:
            acc = acc + values(p + 2)
        if 0 <= p + 1 < n_pages_step:
            run = weights(p + 1, run)
        if p >= 0:
            scores(p)
    car_scr[...] = run
    acc_scr[...] = acc

    @pl.when(j == nj - 1)
    def _():
        r = lax.broadcasted_iota(jnp.int32, (rows, d_att), 0)
        c = lax.broadcasted_iota(jnp.int32, (rows, d_att), 1)
        sel = jnp.where(_same_head(r, c, t_new), acc, 0.0)
        out = sel[0:t_new, :]
        for h in range(1, N_HEADS):
            out = out + sel[h * t_new:(h + 1) * t_new, :]
        o_ref[0] = out


def _suffix_and_total(n):
    return jnp.concatenate([-_suffix_ones(n), -jnp.ones((n, n), BF16)], axis=1)


def _sattn_call(page_table, q, k_new, v_new, bias_rows, cache_kt, cache_vt, *, layer):
    bsz, t_new, d_att = q.shape
    n_pages = page_table.shape[1]
    pps = PAGES_PER_STEP
    n_steps = n_pages // pps
    rows = N_HEADS * t_new

    def page_map(p):
        return lambda b, j, pt: (layer, pt[b, (n_steps - 1 - j) * pps + p], 0, 0, 0)

    per_b = lambda b, j, pt: (b, 0, 0)
    const2 = lambda b, j, pt: (0, 0)
    page_specs = [pl.BlockSpec((None, None, N_HEADS, HEAD_DIM, PAGE_SIZE), page_map(p)) for p in range(pps)]
    kern = functools.partial(_sattn_kernel, n_pages_step=pps, t_new=t_new, d_att=d_att)
    return pl.pallas_call(
        kern,
        grid_spec=pltpu.PrefetchScalarGridSpec(
            num_scalar_prefetch=1,
            grid=(bsz, n_steps),
            in_specs=[
                pl.BlockSpec((1, t_new, d_att), per_b),
                pl.BlockSpec((1, t_new, d_att), per_b),
                pl.BlockSpec((1, t_new, d_att), per_b),
                pl.BlockSpec((rows, PAGE_SIZE), const2),
                pl.BlockSpec((PAGE_SIZE, 2 * PAGE_SIZE), const2),
            ] + page_specs + page_specs,
            out_specs=pl.BlockSpec((1, t_new, d_att), per_b),
            scratch_shapes=[
                pltpu.VMEM((rows, d_att), BF16),
                pltpu.VMEM((rows, d_att), F32),
                pltpu.VMEM((rows, PAGE_SIZE), F32),
                pltpu.VMEM((pps, rows, PAGE_SIZE), F32),
                pltpu.VMEM((pps, rows, PAGE_SIZE), BF16),
                pltpu.VMEM((pps, rows, PAGE_SIZE), BF16),
            ],
        ),
        out_shape=jax.ShapeDtypeStruct((bsz, t_new, d_att), F32),
        compiler_params=_params(2),
        name="sample_attention",
    )(page_table, q, k_new, v_new, bias_rows, _suffix_and_total(PAGE_SIZE),
      *([cache_kt] * pps), *([cache_vt] * pps))


def _post_kernel(x_ref, yrec_ref, o_ref, sgb_ref, gate_ref, wout_ref, lng_ref, lnb_ref, out_ref,
                 *, d_rec, alpha):
    yatt = (o_ref[0] * sgb_ref[0]).astype(BF16)
    y = jnp.dot(yrec_ref[0], wout_ref[0:d_rec, :], preferred_element_type=F32)
    y = y + jnp.dot(yatt, wout_ref[d_rec:, :], preferred_element_type=F32)
    y = y * gate_ref[0]
    s = alpha * x_ref[0] + y
    out_ref[0] = _layer_norm(s) * lng_ref[...] + lnb_ref[...]


def _post_call(x, yrec, o, sgb, gate, wout, lng, lnb, *, rows, alpha):
    bsz, seq, d = x.shape
    d_rec = yrec.shape[2]
    d_att = o.shape[2]
    kern = functools.partial(_post_kernel, d_rec=d_rec, alpha=alpha)
    tile = lambda b, t: (b, t, 0)
    const2 = lambda b, t: (0, 0)
    return pl.pallas_call(
        kern,
        grid=(bsz, seq // rows),
        in_specs=[
            pl.BlockSpec((1, rows, d), tile),
            pl.BlockSpec((1, rows, d_rec), tile),
            pl.BlockSpec((1, rows, d_att), tile),
            pl.BlockSpec((1, rows, d_att), tile),
            pl.BlockSpec((1, 1, d), lambda b, t: (b, 0, 0)),
            pl.BlockSpec((d_rec + d_att, d), const2),
            pl.BlockSpec((1, d), const2),
            pl.BlockSpec((1, d), const2),
        ],
        out_specs=pl.BlockSpec((1, rows, d), tile),
        out_shape=jax.ShapeDtypeStruct((bsz, seq, d), F32),
        compiler_params=_params(2),
        name="post_mixer",
    )(x, yrec, o, sgb, gate, wout, lng, lnb)


def _suffix_ones(n):
    s = lax.broadcasted_iota(jnp.int32, (n, n), 0)
    j = lax.broadcasted_iota(jnp.int32, (n, n), 1)
    return (s >= j).astype(BF16)


def _block_diag(w):
    n, d, e = w.shape
    eye = jnp.eye(n, dtype=w.dtype)
    return (eye[:, None, :, None] * w[:, :, None, :]).reshape(n * d, n * e)


def kernel(x_prompt, x_sample, cache_k, cache_v, state_h, state_conv, page_table, c_prompt, c_sample,
           w_ada, b_ada, w_in, conv_w, conv_b, w_r, b_r, w_i, b_i, lam, sb_bias, w_out, ln_g, ln_b):
    depth = w_ada.shape[0]
    bp, seq, d = x_prompt.shape
    bs, t_new, _ = x_sample.shape
    d_rec = conv_w.shape[2]
    d_att = N_HEADS * HEAD_DIM
    alpha = (2 * depth) ** 0.25

    n_c = bp + bs
    pad_rows = (-n_c) % SUBLANES
    c_all = jnp.concatenate([c_prompt, c_sample, jnp.zeros((pad_rows, d), F32)], axis=0)
    mod = _modulation(c_all, w_ada, b_ada)

    u_prompt = -_suffix_ones(ATT_BLOCK)
    cache_kt = jnp.transpose(cache_k, (0, 1, 3, 4, 2))
    cache_vt = jnp.transpose(cache_v, (0, 1, 3, 4, 2))
    zeros_hist = jnp.zeros((bp, SUBLANES, d_rec), F32)
    zeros_h = jnp.zeros((bp, 1, d_rec), F32)
    hist_pad = jnp.zeros((depth, bs, SUBLANES - (CONV_W - 1), d_rec), F32)
    state_conv8 = jnp.concatenate([hist_pad, state_conv], axis=2)

    xp, xs = x_prompt, x_sample
    outs = [[] for _ in range(8)]
    for l in range(depth):
        shift, scale, gate = mod[l, :, :d], mod[l, :, d:2 * d], mod[l, :, 2 * d:]
        win = w_in[l].astype(BF16)
        wg = jnp.concatenate([_block_diag(w_r[l]), _block_diag(w_i[l])], axis=1).astype(BF16)
        bg = jnp.concatenate([b_r[l], b_i[l]])[None, :]
        wout = w_out[l].astype(BF16)
        weights = (win, conv_w[l], conv_b[l][None, :], wg, bg, lam[l][None, :])
        lng, lnb = ln_g[l][None, :], ln_b[l][None, :]

        sl = slice(0, bp)
        k, v, qa, ka, va, yrec, sgb, hl, cv = _pre_call(
            xp, shift[sl, None, :], scale[sl, None, :], *weights, zeros_hist, zeros_h,
            rows=PRE_ROWS, reset_first=True, q_dtype=BF16, q_scale=LOG2E / math.sqrt(HEAD_DIM))
        o = _attn_call_t(qa, ka, va, _bias_columns(sb_bias[l], ATT_BLOCK), u_prompt, blk=ATT_BLOCK, pairs=ATT_PAIRS)
        xp = _post_call(xp, yrec, o, sgb, gate[sl, None, :], wout, lng, lnb, rows=POST_ROWS, alpha=alpha)
        outs[0].append(k.reshape(bp, seq, N_HEADS, HEAD_DIM))
        outs[1].append(v.reshape(bp, seq, N_HEADS, HEAD_DIM))
        outs[2].append(hl[:, 0, :])
        outs[3].append(cv[:, SUBLANES - (CONV_W - 1):, :])

        sl = slice(bp, bp + bs)
        k, v, qa, _, _, yrec, sgb, hl, cv = _pre_call(
            xs, shift[sl, None, :], scale[sl, None, :], *weights, state_conv8[l], state_h[l][:, None, :],
            rows=t_new, reset_first=False, q_dtype=F32, q_scale=LOG2E / math.sqrt(HEAD_DIM))
        bias_rows = jnp.broadcast_to(jnp.repeat(sb_bias[l] * LOG2E, t_new)[:, None], (N_HEADS * t_new, LANES))
        o = _sattn_call(page_table, qa, k, v, bias_rows, cache_kt, cache_vt, layer=l)
        xs = _post_call(xs, yrec, o, sgb, gate[sl, None, :], wout, lng, lnb, rows=t_new, alpha=alpha)
        outs[4].append(k.reshape(bs, t_new, N_HEADS, HEAD_DIM))
        outs[5].append(v.reshape(bs, t_new, N_HEADS, HEAD_DIM))
        outs[6].append(hl[:, 0, :])
        outs[7].append(cv[:, SUBLANES - (CONV_W - 1):, :])

    stacked = [jnp.stack(o_) for o_ in outs]
    return (xp, xs, *stacked)
```
